```python
import math
import jax, jax.numpy as jnp
from jax import lax
import numpy as np

D_MODEL = 1024
BATCH = 2
SEQ = 8192
DEPTH = 2

N_EVEN = (DEPTH + 1) // 2
N_ODD = DEPTH // 2
NORM_EPS = 1e-6
Q_BLOCK = 128
D_FF = 2816
S5_WIDTH = D_MODEL // 2
S5_GROUP_CH = 16
S5_GROUPS = S5_WIDTH // S5_GROUP_CH
S5_STATE = 64
S5_DT_MIN = 1e-3
S5_DT_MAX = 1e-1
MLA_HEADS = 8
MLA_NOPE = 64
MLA_ROPE = 32
MLA_V = 64
MLA_Q_LORA = 384
MLA_KV_LORA = 256
MLA_ROPE_THETA = 10000.0
EVEN_IN = S5_WIDTH + MLA_Q_LORA + MLA_KV_LORA + MLA_ROPE
EVEN_MIX = S5_WIDTH + MLA_HEADS * MLA_V
DIFF_HEADS = 8
DIFF_HEAD_DIM = D_MODEL // DIFF_HEADS // 2
DIFF_ROT = DIFF_HEAD_DIM // 4
ROPE_THETA = 500000.0
ODD_IN = 3 * DIFF_HEADS * 2 * DIFF_HEAD_DIM
ODD_MIX = DIFF_HEADS * 2 * DIFF_HEAD_DIM

kernel_name = "hybrid_s5_mla_diffattn_macaron"

F32 = jnp.float32


def _rmsnorm(x, g):
    xf = x.astype(F32)
    y = xf * lax.rsqrt(jnp.mean(xf * xf, axis=-1, keepdims=True) + NORM_EPS)
    return (y * g.astype(F32)).astype(x.dtype)


def _swiglu(h, w_gu, w_down):
    g, u = jnp.split(h @ w_gu, 2, axis=-1)
    return (jax.nn.silu(g) * u) @ w_down


def _rope_tables(positions, rot_dim, theta):
    half = rot_dim // 2
    inv = theta ** (-jnp.arange(half, dtype=F32) * 2.0 / rot_dim)
    ang = positions.astype(F32)[..., None] * inv
    return jnp.cos(ang), jnp.sin(ang)


def _apply_rope(x, cos, sin):
    half = cos.shape[-1]
    shape = (cos.shape[0],) + (1,) * (x.ndim - 3) + cos.shape[1:]
    c = cos.reshape(shape)
    s = sin.reshape(shape)
    xf = x.astype(F32)
    x1 = xf[..., :half]
    x2 = xf[..., half:2 * half]
    out = jnp.concatenate([x1 * c - x2 * s, x2 * c + x1 * s, xf[..., 2 * half:]], axis=-1)
    return out.astype(x.dtype)


def _causal_mask(i, s):
    qpos = i * Q_BLOCK + jnp.arange(Q_BLOCK)
    return jnp.arange(s)[None, :] <= qpos[:, None]


def _causal_attention(q, k, v):
    bsz, nh, s, dk = q.shape
    dv = v.shape[-1]
    nb = s // Q_BLOCK
    scale = dk ** -0.5
    qb = q.reshape(bsz, nh, nb, Q_BLOCK, dk).transpose(2, 0, 1, 3, 4)

    def block(args):
        qi, i = args
        sc = jnp.einsum('bhqd,bhkd->bhqk', qi, k).astype(F32) * scale
        sc = jnp.where(_causal_mask(i, s), sc, -jnp.inf)
        p = jax.nn.softmax(sc, axis=-1)
        return jnp.einsum('bhqk,bhkd->bhqd', p.astype(v.dtype), v)

    o = lax.map(block, (qb, jnp.arange(nb)))
    return o.transpose(1, 2, 0, 3, 4).reshape(bsz, nh, s, dv)


def _diff_attention(q, k, v, lam):
    bsz, nh, _, s, d = q.shape
    dv = v.shape[-1]
    nb = s // Q_BLOCK
    scale = d ** -0.5
    qb = q.reshape(bsz, nh, 2, nb, Q_BLOCK, d).transpose(3, 0, 1, 2, 4, 5)

    def block(args):
        qi, i = args
        sc = jnp.einsum('bhmqd,bhmkd->bhmqk', qi, k).astype(F32) * scale
        sc = jnp.where(_causal_mask(i, s), sc, -jnp.inf)
        p = jax.nn.softmax(sc, axis=-1)
        w = p[:, :, 0] - lam * p[:, :, 1]
        return jnp.einsum('bhqk,bhkd->bhqd', w.astype(v.dtype), v)

    o = lax.map(block, (qb, jnp.arange(nb)))
    return o.transpose(1, 2, 0, 3, 4).reshape(bsz, nh, s, dv)


def _s5(u, lam_re, lam_im, log_dt, b_re, b_im, c_re, c_im, d_skip, w_glu, b_glu):
    bsz, s, _ = u.shape
    uf = u.astype(F32).reshape(bsz, s, S5_GROUPS, S5_GROUP_CH)
    dt = jnp.exp(log_dt.astype(F32))[:, None]
    lr = lam_re.astype(F32)
    li = lam_im.astype(F32)
    mag = jnp.exp(lr * dt)
    ang = li * dt
    ab_re = mag * jnp.cos(ang)
    ab_im = mag * jnp.sin(ang)
    den = lr * lr + li * li
    nr = ab_re - 1.0
    f_re = (nr * lr + ab_im * li) / den
    f_im = (ab_im * lr - nr * li) / den
    br = b_re.astype(F32)
    bi = b_im.astype(F32)
    bb_re = f_re[..., None] * br - f_im[..., None] * bi
    bb_im = f_re[..., None] * bi + f_im[..., None] * br
    bu_re = jnp.einsum('bsgh,gph->sbgp', uf, bb_re)
    bu_im = jnp.einsum('bsgh,gph->sbgp', uf, bb_im)
    a_re = jnp.broadcast_to(ab_re[None, None], (s, 1, S5_GROUPS, S5_STATE))
    a_im = jnp.broadcast_to(ab_im[None, None], (s, 1, S5_GROUPS, S5_STATE))

    def combine(e_i, e_j):
        ar_i, ai_i, xr_i, xi_i = e_i
        ar_j, ai_j, xr_j, xi_j = e_j
        return (ar_j * ar_i - ai_j * ai_i,
                ar_j * ai_i + ai_j * ar_i,
                ar_j * xr_i - ai_j * xi_i + xr_j,
                ar_j * xi_i + ai_j * xr_i + xi_j)

    _, _, x_re, x_im = lax.associative_scan(combine, (a_re, a_im, bu_re, bu_im), axis=0)
    y = (jnp.einsum('sbgp,ghp->bsgh', x_re, c_re.astype(F32))
         - jnp.einsum('sbgp,ghp->bsgh', x_im, c_im.astype(F32))
         + d_skip.astype(F32) * uf)
    g = jax.nn.gelu(y.reshape(bsz, s, S5_WIDTH))
    out = g * jax.nn.sigmoid(g @ w_glu.astype(F32) + b_glu.astype(F32))
    return out.astype(u.dtype)


def _even_mixer(h, cos_m, sin_m, w_in, lam_re, lam_im, log_dt, b_re, b_im, c_re, c_im,
                d_skip, w_glu, b_glu, q_norm, w_uq, kv_norm, w_ukv, w_out):
    bsz, s, _ = h.shape
    proj = h @ w_in
    u, c_q, c_kv, k_rope = jnp.split(
        proj, [S5_WIDTH, S5_WIDTH + MLA_Q_LORA, S5_WIDTH + MLA_Q_LORA + MLA_KV_LORA], axis=-1)
    s5_out = _s5(u, lam_re, lam_im, log_dt, b_re, b_im, c_re, c_im, d_skip, w_glu, b_glu)
    q = (_rmsnorm(c_q, q_norm) @ w_uq).reshape(bsz, s, MLA_HEADS, MLA_NOPE + MLA_ROPE).transpose(0, 2, 1, 3)
    kv = (_rmsnorm(c_kv, kv_norm) @ w_ukv).reshape(bsz, s, MLA_HEADS, MLA_NOPE + MLA_V).transpose(0, 2, 1, 3)
    k_nope, v = kv[..., :MLA_NOPE], kv[..., MLA_NOPE:]
    q_pe = _apply_rope(q[..., MLA_NOPE:], cos_m, sin_m)
    k_pe = _apply_rope(k_rope, cos_m, sin_m)[:, None]
    q_full = jnp.concatenate([q[..., :MLA_NOPE], q_pe], axis=-1)
    k_full = jnp.concatenate([k_nope, jnp.broadcast_to(k_pe, (bsz, MLA_HEADS, s, MLA_ROPE))], axis=-1)
    o = _causal_attention(q_full, k_full, v)
    o = o.transpose(0, 2, 1, 3).reshape(bsz, s, MLA_HEADS * MLA_V)
    return jnp.concatenate([s5_out, o], axis=-1) @ w_out


def _odd_mixer(h, cos_t, sin_t, layer_idx, w_in, lq1, lk1, lq2, lk2, subln, w_out):
    bsz, s, _ = h.shape
    q, k, v = jnp.split(h @ w_in, 3, axis=-1)
    q = q.reshape(bsz, s, DIFF_HEADS, 2, DIFF_HEAD_DIM).transpose(0, 2, 3, 1, 4)
    k = k.reshape(bsz, s, DIFF_HEADS, 2, DIFF_HEAD_DIM).transpose(0, 2, 3, 1, 4)
    v = v.reshape(bsz, s, DIFF_HEADS, 2 * DIFF_HEAD_DIM).transpose(0, 2, 1, 3)
    q = _apply_rope(q, cos_t, sin_t)
    k = _apply_rope(k, cos_t, sin_t)
    lam_init = 0.8 - 0.6 * math.exp(-0.3 * layer_idx)
    lam = (jnp.exp(jnp.sum(lq1.astype(F32) * lk1.astype(F32)))
           - jnp.exp(jnp.sum(lq2.astype(F32) * lk2.astype(F32))) + lam_init)
    o = _diff_attention(q, k, v, lam)
    o = _rmsnorm(o, subln) * (1.0 - lam_init)
    o = o.transpose(0, 2, 1, 3).reshape(bsz, s, ODD_MIX)
    return o @ w_out


def setup_inputs(seed: int = 0) -> dict:
    key = jax.random.key(seed)
    ks = iter(jax.random.split(key, 32))

    def nrm(shape, scale):
        return jax.random.normal(next(ks), shape, F32) * scale

    def gain(shape):
        return 1.0 + nrm(shape, 0.02)

    G, P, H = S5_GROUPS, S5_STATE, S5_GROUP_CH
    x = nrm((BATCH, SEQ, D_MODEL), 1.0)
    positions = jnp.broadcast_to(jnp.arange(SEQ, dtype=jnp.int32)[None, :], (BATCH, SEQ))
    ffn_norm = gain((DEPTH, 2, D_MODEL))
    ffn_w_gu = nrm((DEPTH, 2, D_MODEL, 2 * D_FF), D_MODEL ** -0.5)
    ffn_w_down = nrm((DEPTH, 2, D_FF, D_MODEL), D_FF ** -0.5)
    ev_norm = gain((N_EVEN, D_MODEL))
    ev_w_in = nrm((N_EVEN, D_MODEL, EVEN_IN), D_MODEL ** -0.5)
    s5_lambda_re = -0.5 + nrm((N_EVEN, G, P), 0.01)
    s5_lambda_im = jnp.pi * jnp.arange(P, dtype=F32)[None, None, :] + nrm((N_EVEN, G, P), 0.01)
    s5_log_dt = jax.random.uniform(next(ks), (N_EVEN, G), F32,
                                   minval=math.log(S5_DT_MIN), maxval=math.log(S5_DT_MAX))
    s5_b_re = nrm((N_EVEN, G, P, H), (2 * H) ** -0.5)
    s5_b_im = nrm((N_EVEN, G, P, H), (2 * H) ** -0.5)
    s5_c_re = nrm((N_EVEN, G, H, P), P ** -0.5)
    s5_c_im = nrm((N_EVEN, G, H, P), P ** -0.5)
    s5_d = nrm((N_EVEN, G, H), 1.0)
    s5_w_glu = nrm((N_EVEN, S5_WIDTH, S5_WIDTH), S5_WIDTH ** -0.5)
    s5_b_glu = nrm((N_EVEN, S5_WIDTH), 0.01)
    mla_q_norm = gain((N_EVEN, MLA_Q_LORA))
    mla_w_uq = nrm((N_EVEN, MLA_Q_LORA, MLA_HEADS * (MLA_NOPE + MLA_ROPE)), MLA_Q_LORA ** -0.5)
    mla_kv_norm = gain((N_EVEN, MLA_KV_LORA))
    mla_w_ukv = nrm((N_EVEN, MLA_KV_LORA, MLA_HEADS * (MLA_NOPE + MLA_V)), MLA_KV_LORA ** -0.5)
    ev_w_out = nrm((N_EVEN, EVEN_MIX, D_MODEL), EVEN_MIX ** -0.5)
    od_norm = gain((N_ODD, D_MODEL))
    od_w_in = nrm((N_ODD, D_MODEL, ODD_IN), D_MODEL ** -0.5)
    diff_lq1 = nrm((N_ODD, DIFF_HEAD_DIM), 0.1)
    diff_lk1 = nrm((N_ODD, DIFF_HEAD_DIM), 0.1)
    diff_lq2 = nrm((N_ODD, DIFF_HEAD_DIM), 0.1)
    diff_lk2 = nrm((N_ODD, DIFF_HEAD_DIM), 0.1)
    diff_subln = gain((N_ODD, 2 * DIFF_HEAD_DIM))
    od_w_out = nrm((N_ODD, ODD_MIX, D_MODEL), ODD_MIX ** -0.5)
    final_norm = gain((D_MODEL,))
    return {"x": x, "positions": positions, "ffn_norm": ffn_norm, "ffn_w_gu": ffn_w_gu,
            "ffn_w_down": ffn_w_down, "ev_norm": ev_norm, "ev_w_in": ev_w_in,
            "s5_lambda_re": s5_lambda_re, "s5_lambda_im": s5_lambda_im, "s5_log_dt": s5_log_dt,
            "s5_b_re": s5_b_re, "s5_b_im": s5_b_im, "s5_c_re": s5_c_re, "s5_c_im": s5_c_im,
            "s5_d": s5_d, "s5_w_glu": s5_w_glu, "s5_b_glu": s5_b_glu, "mla_q_norm": mla_q_norm,
            "mla_w_uq": mla_w_uq, "mla_kv_norm": mla_kv_norm, "mla_w_ukv": mla_w_ukv,
            "ev_w_out": ev_w_out, "od_norm": od_norm, "od_w_in": od_w_in, "diff_lq1": diff_lq1,
            "diff_lk1": diff_lk1, "diff_lq2": diff_lq2, "diff_lk2": diff_lk2,
            "diff_subln": diff_subln, "od_w_out": od_w_out, "final_norm": final_norm}


def reference(x, positions, ffn_norm, ffn_w_gu, ffn_w_down, ev_norm, ev_w_in, s5_lambda_re,
              s5_lambda_im, s5_log_dt, s5_b_re, s5_b_im, s5_c_re, s5_c_im, s5_d, s5_w_glu,
              s5_b_glu, mla_q_norm, mla_w_uq, mla_kv_norm, mla_w_ukv, ev_w_out, od_norm, od_w_in,
              diff_lq1, diff_lk1, diff_lq2, diff_lk2, diff_subln, od_w_out, final_norm):
    cos_t, sin_t = _rope_tables(positions, DIFF_ROT, ROPE_THETA)
    cos_m, sin_m = _rope_tables(positions, MLA_ROPE, MLA_ROPE_THETA)
    for l in range(DEPTH):
        x = x + 0.5 * _swiglu(_rmsnorm(x, ffn_norm[l, 0]), ffn_w_gu[l, 0], ffn_w_down[l, 0])
        i = l // 2
        if l % 2 == 0:
            h = _rmsnorm(x, ev_norm[i])
            x = x + _even_mixer(h, cos_m, sin_m, ev_w_in[i], s5_lambda_re[i], s5_lambda_im[i],
                                s5_log_dt[i], s5_b_re[i], s5_b_im[i], s5_c_re[i], s5_c_im[i],
                                s5_d[i], s5_w_glu[i], s5_b_glu[i], mla_q_norm[i], mla_w_uq[i],
                                mla_kv_norm[i], mla_w_ukv[i], ev_w_out[i])
        else:
            h = _rmsnorm(x, od_norm[i])
            x = x + _odd_mixer(h, cos_t, sin_t, l, od_w_in[i], diff_lq1[i], diff_lk1[i],
                               diff_lq2[i], diff_lk2[i], diff_subln[i], od_w_out[i])
        x = x + 0.5 * _swiglu(_rmsnorm(x, ffn_norm[l, 1]), ffn_w_gu[l, 1], ffn_w_down[l, 1])
    return _rmsnorm(x, final_norm)
```

```python
import functools
import math

import jax
import jax.numpy as jnp
from jax import lax
from jax.experimental import pallas as pl
from jax.experimental.pallas import tpu as pltpu

F32 = jnp.float32
BF16 = jnp.bfloat16

D_MODEL = 1024
NORM_EPS = 1e-6
D_FF = 2816
S5_WIDTH = 512
S5_GROUP_CH = 16
S5_GROUPS = 32
S5_STATE = 64
S5_FEATS = S5_GROUPS * S5_STATE
MLA_HEADS = 8
MLA_NOPE = 64
MLA_ROPE = 32
MLA_V = 64
MLA_Q_LORA = 384
MLA_KV_LORA = 256
MLA_ROPE_THETA = 10000.0
DIFF_HEADS = 8
DIFF_HEAD_DIM = 64
DIFF_ROT = 16
ROPE_THETA = 500000.0

LANES = 128
SUBLANES = 8
VMEM_LIMIT = 56 * 1024 * 1024

FFN_TM = 512
FFN_CHUNKS = 2
PREP_TM = 512
S5_TS = 256
S5_PITCH = S5_TS + SUBLANES
S5_ROWS = S5_FEATS // LANES
ATT_TQ = 512
ATT_TK = 512


def _rms(x, g):
    ms = jnp.mean(x * x, axis=-1, keepdims=True)
    return x * lax.rsqrt(ms + NORM_EPS) * g


def _const_spec(shape):
    nd = len(shape)
    return pl.BlockSpec(shape, lambda *_: (0,) * nd, pipeline_mode=pl.Buffered(1))


def _params(sem):
    return pltpu.CompilerParams(dimension_semantics=sem, vmem_limit_bytes=VMEM_LIMIT)


def _ffn_body(*refs, n_mix, final):
    x_ref = refs[0]
    mix_refs = refs[1:1 + 2 * n_mix]
    g_ref, wgu_ref, wd_ref = refs[1 + 2 * n_mix:4 + 2 * n_mix]
    gf_ref = refs[4 + 2 * n_mix] if final else None
    o_ref = refs[-1]

    x = x_ref[...]
    for i in range(n_mix):
        x = x + jnp.dot(mix_refs[2 * i][...], mix_refs[2 * i + 1][...], preferred_element_type=F32)
    h = _rms(x, g_ref[...]).astype(BF16)
    ck = D_FF // FFN_CHUNKS
    y = None
    for c in range(FFN_CHUNKS):
        g = jnp.dot(h, wgu_ref[:, c * ck:(c + 1) * ck], preferred_element_type=F32)
        u = jnp.dot(h, wgu_ref[:, D_FF + c * ck:D_FF + (c + 1) * ck], preferred_element_type=F32)
        a = (g * jax.nn.sigmoid(g) * u).astype(BF16)
        d = jnp.dot(a, wd_ref[c * ck:(c + 1) * ck, :], preferred_element_type=F32)
        y = d if y is None else y + d
    out = x + 0.5 * y
    if final:
        out = _rms(out, gf_ref[...])
    o_ref[...] = out


def _ffn(x, mixes, g, wgu, wd, gfinal=None):
    t = x.shape[0]
    n_mix = len(mixes)
    final = gfinal is not None
    in_specs = [pl.BlockSpec((FFN_TM, D_MODEL), lambda i: (i, 0))]
    args = [x]
    for m, w in mixes:
        in_specs += [pl.BlockSpec((FFN_TM, m.shape[1]), lambda i: (i, 0)), _const_spec(w.shape)]
        args += [m, w]
    in_specs += [_const_spec((1, D_MODEL)), _const_spec(wgu.shape), _const_spec(wd.shape)]
    args += [g.reshape(1, D_MODEL), wgu, wd]
    if final:
        in_specs.append(_const_spec((1, D_MODEL)))
        args.append(gfinal.reshape(1, D_MODEL))
    return pl.pallas_call(
        functools.partial(_ffn_body, n_mix=n_mix, final=final),
        grid=(t // FFN_TM,),
        in_specs=in_specs,
        out_specs=pl.BlockSpec((FFN_TM, D_MODEL), lambda i: (i, 0)),
        out_shape=jax.ShapeDtypeStruct((t, D_MODEL), F32),
        compiler_params=_params(("parallel",)),
        name="ffn",
    )(*args)


def _rope(blk, cos, sin, half, first_lo, period):
    lane = lax.broadcasted_iota(jnp.int32, blk.shape, 1) % period
    fwd = pltpu.roll(blk, LANES - half, 1)
    bwd = pltpu.roll(blk, half, 1)
    rot = jnp.where(lane < first_lo + half, fwd, bwd)
    return blk * cos + rot * sin


def _even_prep_body(x_ref, g_ref, win_ref, qn_ref, wuq_ref, kvn_ref, wuk_ref, wuv_ref, cos_ref, sin_ref,
                    u_ref, q_ref, k_ref, v_ref):
    h = _rms(x_ref[...], g_ref[...]).astype(BF16)
    proj = jnp.dot(h, win_ref[...], preferred_element_type=F32)
    o_cq = S5_WIDTH
    o_ckv = o_cq + MLA_Q_LORA
    o_kr = o_ckv + MLA_KV_LORA
    u_ref[...] = proj[:, :o_cq]
    cq = proj[:, o_cq:o_ckv]
    ckv = proj[:, o_ckv:o_kr]
    kr = proj[:, o_kr:o_kr + LANES]
    cos = cos_ref[...]
    sin = sin_ref[...]
    half = MLA_ROPE // 2
    scale = (MLA_NOPE + MLA_ROPE) ** -0.5
    q = jnp.dot(_rms(cq, qn_ref[...]).astype(BF16), wuq_ref[...], preferred_element_type=F32)
    ckn = _rms(ckv, kvn_ref[...]).astype(BF16)
    kn = jnp.dot(ckn, wuk_ref[...], preferred_element_type=F32)
    v_ref[...] = jnp.dot(ckn, wuv_ref[...], preferred_element_type=F32).astype(BF16)
    kpe = _rope(kr, cos, sin, half, MLA_NOPE, LANES)
    for hd in range(MLA_HEADS):
        sl = slice(hd * LANES, (hd + 1) * LANES)
        q_ref[:, sl] = (_rope(q[:, sl], cos, sin, half, MLA_NOPE, LANES) * scale).astype(BF16)
        k_ref[:, sl] = (kn[:, sl] + kpe).astype(BF16)


def _even_prep(x, g, win, qn, wuq, kvn, wuk, wuv, cos, sin):
    t = x.shape[0]
    tm = PREP_TM
    row = lambda w: pl.BlockSpec((tm, w), lambda i: (i, 0))
    return pl.pallas_call(
        _even_prep_body,
        grid=(t // tm,),
        in_specs=[row(D_MODEL), _const_spec((1, D_MODEL)), _const_spec(win.shape),
                  _const_spec((1, MLA_Q_LORA)), _const_spec(wuq.shape),
                  _const_spec((1, MLA_KV_LORA)), _const_spec(wuk.shape), _const_spec(wuv.shape),
                  row(LANES), row(LANES)],
        out_specs=[row(S5_WIDTH), row(MLA_HEADS * LANES), row(MLA_HEADS * LANES), row(MLA_HEADS * MLA_V)],
        out_shape=[jax.ShapeDtypeStruct((t, S5_WIDTH), F32),
                   jax.ShapeDtypeStruct((t, MLA_HEADS * LANES), BF16),
                   jax.ShapeDtypeStruct((t, MLA_HEADS * LANES), BF16),
                   jax.ShapeDtypeStruct((t, MLA_HEADS * MLA_V), BF16)],
        compiler_params=_params(("parallel",)),
        name="even_prep",
    )(x, g.reshape(1, -1), win, qn.reshape(1, -1), wuq, kvn.reshape(1, -1), wuk, wuv, cos, sin)


def _odd_prep_body(x_ref, g_ref, win_ref, cos_ref, sin_ref, q_ref, k_ref, v_ref):
    h = _rms(x_ref[...], g_ref[...]).astype(BF16)
    cos = cos_ref[...]
    sin = sin_ref[...]
    half = DIFF_ROT // 2
    width = DIFF_HEADS * 2 * DIFF_HEAD_DIM
    scale = DIFF_HEAD_DIM ** -0.5
    q = jnp.dot(h, win_ref[:, :width], preferred_element_type=F32)
    k = jnp.dot(h, win_ref[:, width:2 * width], preferred_element_type=F32)
    v_ref[...] = jnp.dot(h, win_ref[:, 2 * width:], preferred_element_type=F32).astype(BF16)
    for c in range(width // LANES):
        sl = slice(c * LANES, (c + 1) * LANES)
        q_ref[:, sl] = (_rope(q[:, sl], cos, sin, half, 0, DIFF_HEAD_DIM) * scale).astype(BF16)
        k_ref[:, sl] = _rope(k[:, sl], cos, sin, half, 0, DIFF_HEAD_DIM).astype(BF16)


def _odd_prep(x, g, win, cos, sin):
    t = x.shape[0]
    tm = PREP_TM
    width = DIFF_HEADS * 2 * DIFF_HEAD_DIM
    row = lambda w: pl.BlockSpec((tm, w), lambda i: (i, 0))
    return pl.pallas_call(
        _odd_prep_body,
        grid=(t // tm,),
        in_specs=[row(D_MODEL), _const_spec((1, D_MODEL)), _const_spec(win.shape), row(LANES), row(LANES)],
        out_specs=[row(width)] * 3,
        out_shape=[jax.ShapeDtypeStruct((t, width), BF16)] * 3,
        compiler_params=_params(("parallel",)),
        name="odd_prep",
    )(x, g.reshape(1, -1), win, cos, sin)


def _s5_body(u_ref, ar_ref, ai_ref, bre_ref, bim_ref, c_ref, d_ref, wglu_ref, bglu_ref, o_ref,
             xr_ref, xi_ref, sr_ref, si_ref):
    nb = u_ref.shape[0]

    @pl.when(pl.program_id(0) == 0)
    def _():
        sr_ref[...] = jnp.zeros_like(sr_ref)
        si_ref[...] = jnp.zeros_like(si_ref)

    for b in range(nb):
        ub = u_ref[b].astype(BF16)
        bur = jnp.dot(ub, bre_ref[...], preferred_element_type=F32)
        bui = jnp.dot(ub, bim_ref[...], preferred_element_type=F32)
        for j in range(S5_ROWS):
            xr_ref[b, j * S5_PITCH:j * S5_PITCH + S5_TS, :] = bur[:, j * LANES:(j + 1) * LANES]
            xi_ref[b, j * S5_PITCH:j * S5_PITCH + S5_TS, :] = bui[:, j * LANES:(j + 1) * LANES]

    ar = ar_ref[...]
    ai = ai_ref[...]

    def step(t, carry):
        new = []
        for b in range(nb):
            pr, pi = carry[2 * b], carry[2 * b + 1]
            idx = pl.ds(t, S5_ROWS, stride=S5_PITCH)
            nr = ar * pr - ai * pi + xr_ref[b, idx, :]
            ni = ar * pi + ai * pr + xi_ref[b, idx, :]
            xr_ref[b, idx, :] = nr
            xi_ref[b, idx, :] = ni
            new += [nr, ni]
        return tuple(new)

    init = []
    for b in range(nb):
        init += [sr_ref[b], si_ref[b]]
    fin = lax.fori_loop(0, S5_TS, step, tuple(init), unroll=4)
    for b in range(nb):
        sr_ref[b] = fin[2 * b]
        si_ref[b] = fin[2 * b + 1]

    for b in range(nb):
        parts = [xr_ref[b, j * S5_PITCH:j * S5_PITCH + S5_TS, :].astype(BF16) for j in range(S5_ROWS)]
        parts += [xi_ref[b, j * S5_PITCH:j * S5_PITCH + S5_TS, :].astype(BF16) for j in range(S5_ROWS)]
        xs = jnp.concatenate(parts, axis=1)
        y = jnp.dot(xs, c_ref[...], preferred_element_type=F32) + d_ref[...] * u_ref[b]
        g = 0.5 * y * (1.0 + jnp.tanh(math.sqrt(2.0 / math.pi) * (y + 0.044715 * (y * y * y))))
        z = jnp.dot(g.astype(BF16), wglu_ref[...], preferred_element_type=F32) + bglu_ref[...]
        o_ref[b] = (g * jax.nn.sigmoid(z)).astype(BF16)


def _s5(u, ar, ai, bre, bim, cmat, dskip, wglu, bglu):
    nb, s, _ = u.shape
    blk = pl.BlockSpec((nb, S5_TS, S5_WIDTH), lambda i: (0, i, 0))
    return pl.pallas_call(
        _s5_body,
        grid=(s // S5_TS,),
        in_specs=[blk, _const_spec(ar.shape), _const_spec(ai.shape), _const_spec(bre.shape),
                  _const_spec(bim.shape), _const_spec(cmat.shape), _const_spec((1, S5_WIDTH)),
                  _const_spec(wglu.shape), _const_spec((1, S5_WIDTH))],
        out_specs=blk,
        out_shape=jax.ShapeDtypeStruct((nb, s, S5_WIDTH), BF16),
        scratch_shapes=[pltpu.VMEM((nb, S5_ROWS * S5_PITCH, LANES), F32),
                        pltpu.VMEM((nb, S5_ROWS * S5_PITCH, LANES), F32),
                        pltpu.VMEM((nb, S5_ROWS, LANES), F32),
                        pltpu.VMEM((nb, S5_ROWS, LANES), F32)],
        compiler_params=_params(("arbitrary",)),
        name="s5",
    )(u, ar, ai, bre, bim, cmat, dskip.reshape(1, -1), wglu, bglu.reshape(1, -1))


def _flash_rows(q, k_ref, v_ref, kcol, qi, m_ref, l_ref, acc_ref):
    rows = q.shape[0]
    m_ref[:rows] = jnp.full((rows, 1), -jnp.inf, F32)
    l_ref[:rows] = jnp.zeros((rows, 1), F32)
    acc_ref[:rows] = jnp.zeros((rows, LANES), F32)

    def block(j, masked):
        k = k_ref[0, pl.ds(j * ATT_TK, ATT_TK), kcol * LANES:(kcol + 1) * LANES]
        s = lax.dot_general(q, k, (((1,), (1,)), ((), ())), preferred_element_type=F32)
        if masked:
            rpos = lax.broadcasted_iota(jnp.int32, s.shape, 0) % ATT_TQ
            cpos = lax.broadcasted_iota(jnp.int32, s.shape, 1)
            s = jnp.where(cpos <= rpos, s, -jnp.inf)
        m_old = m_ref[:rows]
        m_new = jnp.maximum(m_old, jnp.max(s, axis=-1, keepdims=True))
        alpha = jnp.exp(m_old - m_new)
        p = jnp.exp(s - m_new)
        l_ref[:rows] = alpha * l_ref[:rows] + jnp.sum(p, axis=-1, keepdims=True)
        pv = jnp.dot(p.astype(BF16), v_ref[0, pl.ds(j * ATT_TK, ATT_TK), :], preferred_element_type=F32)
        acc_ref[:rows] = alpha * acc_ref[:rows] + pv
        m_ref[:rows] = m_new

    def full(j, carry):
        block(j, False)
        return carry

    lax.fori_loop(0, qi, full, 0)
    block(qi, True)
    return acc_ref[:rows] / l_ref[:rows]


def _mla_attn_body(q_ref, k_ref, v_ref, o_ref, m_ref, l_ref, acc_ref):
    qi = pl.program_id(2)
    outs = [_flash_rows(q_ref[0, :, hh * LANES:(hh + 1) * LANES], k_ref, v_ref, hh, qi, m_ref, l_ref, acc_ref)
            for hh in range(2)]
    lane = lax.broadcasted_iota(jnp.int32, outs[0].shape, 1)
    o_ref[0] = jnp.where(lane < MLA_V, outs[0], outs[1]).astype(BF16)


def _mla_attn(q, k, v):
    nb, s, _ = q.shape
    return pl.pallas_call(
        _mla_attn_body,
        grid=(nb, MLA_HEADS // 2, s // ATT_TQ),
        in_specs=[pl.BlockSpec((1, ATT_TQ, 2 * LANES), lambda b, h, i: (b, i, h)),
                  pl.BlockSpec((1, s, 2 * LANES), lambda b, h, i: (b, 0, h)),
                  pl.BlockSpec((1, s, LANES), lambda b, h, i: (b, 0, h))],
        out_specs=pl.BlockSpec((1, ATT_TQ, LANES), lambda b, h, i: (b, i, h)),
        out_shape=jax.ShapeDtypeStruct((nb, s, MLA_HEADS * MLA_V), BF16),
        scratch_shapes=[pltpu.VMEM((ATT_TQ, 1), F32), pltpu.VMEM((ATT_TQ, 1), F32),
                        pltpu.VMEM((ATT_TQ, LANES), F32)],
        compiler_params=_params(("parallel", "parallel", "arbitrary")),
        name="mla_attn",
    )(q, k, v)


def _diff_attn_body(lam_ref, q_ref, k_ref, v_ref, sub_ref, o_ref, m_ref, l_ref, acc_ref, *, out_scale):
    qi = pl.program_id(2)
    q = q_ref[0]
    lane = lax.broadcasted_iota(jnp.int32, q.shape, 1)
    zero = jnp.zeros_like(q)
    q2 = jnp.concatenate([jnp.where(lane < DIFF_HEAD_DIM, q, zero), jnp.where(lane < DIFF_HEAD_DIM, zero, q)], axis=0)
    o = _flash_rows(q2, k_ref, v_ref, 0, qi, m_ref, l_ref, acc_ref)
    o = o[:ATT_TQ] - lam_ref[0, 0] * o[ATT_TQ:]
    o_ref[0] = (_rms(o, sub_ref[...]) * out_scale).astype(BF16)


def _diff_attn(lam, q, k, v, subln, out_scale):
    nb, s, _ = q.shape
    return pl.pallas_call(
        functools.partial(_diff_attn_body, out_scale=out_scale),
        grid=(nb, DIFF_HEADS, s // ATT_TQ),
        in_specs=[pl.BlockSpec(memory_space=pltpu.SMEM),
                  pl.BlockSpec((1, ATT_TQ, LANES), lambda b, h, i: (b, i, h)),
                  pl.BlockSpec((1, s, LANES), lambda b, h, i: (b, 0, h)),
                  pl.BlockSpec((1, s, LANES), lambda b, h, i: (b, 0, h)),
                  pl.BlockSpec((1, LANES), lambda b, h, i: (0, 0))],
        out_specs=pl.BlockSpec((1, ATT_TQ, LANES), lambda b, h, i: (b, i, h)),
        out_shape=jax.ShapeDtypeStruct((nb, s, DIFF_HEADS * 2 * DIFF_HEAD_DIM), BF16),
        scratch_shapes=[pltpu.VMEM((2 * ATT_TQ, 1), F32), pltpu.VMEM((2 * ATT_TQ, 1), F32),
                        pltpu.VMEM((2 * ATT_TQ, LANES), F32)],
        compiler_params=_params(("parallel", "parallel", "arbitrary")),
        name="diff_attn",
    )(lam, q, k, v, subln.reshape(1, -1))


def _rope_table(positions, rot_dim, theta, first_lo, period):
    half = rot_dim // 2
    inv = theta ** (-jnp.arange(half, dtype=F32) * 2.0 / rot_dim)
    ang = positions.astype(F32).reshape(-1, 1) * inv
    cos, sin = jnp.cos(ang), jnp.sin(ang)
    t = ang.shape[0]
    pad_lo = jnp.zeros((t, first_lo), F32)
    pad_hi = jnp.zeros((t, period - first_lo - rot_dim), F32)
    cos_p = jnp.concatenate([pad_lo + 1.0, cos, cos, pad_hi + 1.0], axis=1)
    sin_p = jnp.concatenate([pad_lo, -sin, sin, pad_hi], axis=1)
    reps = LANES // period
    return jnp.tile(cos_p, (1, reps)), jnp.tile(sin_p, (1, reps))


def _s5_discretize(lam_re, lam_im, log_dt, b_re, b_im):
    dt = jnp.exp(log_dt)[:, None]
    mag = jnp.exp(lam_re * dt)
    ang = lam_im * dt
    ab_re = mag * jnp.cos(ang)
    ab_im = mag * jnp.sin(ang)
    den = lam_re * lam_re + lam_im * lam_im
    nr = ab_re - 1.0
    f_re = (nr * lam_re + ab_im * lam_im) / den
    f_im = (ab_im * lam_re - nr * lam_im) / den
    bb_re = f_re[..., None] * b_re - f_im[..., None] * b_im
    bb_im = f_re[..., None] * b_im + f_im[..., None] * b_re
    return ab_re, ab_im, bb_re, bb_im


def _block_diag_in(bb):
    eye = jnp.eye(S5_GROUPS, dtype=F32)
    return jnp.einsum('gph,gk->ghkp', bb, eye).reshape(S5_WIDTH, S5_FEATS)


def _block_diag_out(c):
    eye = jnp.eye(S5_GROUPS, dtype=F32)
    return jnp.einsum('ghp,gk->gpkh', c, eye).reshape(S5_FEATS, S5_WIDTH)


def _pad_heads(w, heads, width, lo, hi):
    kdim = w.shape[0]
    w = w.reshape(kdim, heads, width)[:, :, lo:hi]
    w = jnp.pad(w, ((0, 0), (0, 0), (0, LANES - (hi - lo))))
    return w.reshape(kdim, heads * LANES)


def kernel(x, positions, ffn_norm, ffn_w_gu, ffn_w_down, ev_norm, ev_w_in, s5_lambda_re, s5_lambda_im, s5_log_dt, s5_b_re, s5_b_im, s5_c_re, s5_c_im, s5_d, s5_w_glu, s5_b_glu, mla_q_norm, mla_w_uq, mla_kv_norm, mla_w_ukv, ev_w_out, od_norm, od_w_in, diff_lq1, diff_lk1, diff_lq2, diff_lk2, diff_subln, od_w_out, final_norm):
    nb, s, d = x.shape
    t = nb * s
    xt = x.reshape(t, d)
    wgu = ffn_w_gu.astype(BF16)
    wdn = ffn_w_down.astype(BF16)

    xt = _ffn(xt, [], ffn_norm[0, 0], wgu[0, 0], wdn[0, 0])

    cos_m, sin_m = _rope_table(positions, MLA_ROPE, MLA_ROPE_THETA, MLA_NOPE, LANES)
    w_in = ev_w_in[0]
    o_kr = S5_WIDTH + MLA_Q_LORA + MLA_KV_LORA
    w_kr = jnp.pad(w_in[:, o_kr:], ((0, 0), (MLA_NOPE, LANES - MLA_NOPE - MLA_ROPE)))
    win_p = jnp.concatenate([w_in[:, :o_kr], w_kr], axis=1).astype(BF16)
    wuq_p = _pad_heads(mla_w_uq[0], MLA_HEADS, MLA_NOPE + MLA_ROPE, 0, MLA_NOPE + MLA_ROPE).astype(BF16)
    wuk_p = _pad_heads(mla_w_ukv[0], MLA_HEADS, MLA_NOPE + MLA_V, 0, MLA_NOPE).astype(BF16)
    wuv = mla_w_ukv[0].reshape(MLA_KV_LORA, MLA_HEADS, MLA_NOPE + MLA_V)[:, :, MLA_NOPE:]
    wuv = wuv.reshape(MLA_KV_LORA, MLA_HEADS * MLA_V).astype(BF16)
    u, q, k, v = _even_prep(xt, ev_norm[0], win_p, mla_q_norm[0], wuq_p, mla_kv_norm[0], wuk_p, wuv, cos_m, sin_m)

    ab_re, ab_im, bb_re, bb_im = _s5_discretize(s5_lambda_re[0], s5_lambda_im[0], s5_log_dt[0], s5_b_re[0], s5_b_im[0])
    cmat = jnp.concatenate([_block_diag_out(s5_c_re[0]), -_block_diag_out(s5_c_im[0])], axis=0).astype(BF16)
    s5_out = _s5(u.reshape(nb, s, S5_WIDTH), ab_re.reshape(S5_ROWS, LANES), ab_im.reshape(S5_ROWS, LANES),
                 _block_diag_in(bb_re).astype(BF16), _block_diag_in(bb_im).astype(BF16), cmat,
                 s5_d[0].reshape(-1), s5_w_glu[0].astype(BF16), s5_b_glu[0])

    o = _mla_attn(q.reshape(nb, s, -1), k.reshape(nb, s, -1), v.reshape(nb, s, -1))
    w_out = ev_w_out[0].astype(BF16)
    xt = _ffn(xt, [(s5_out.reshape(t, S5_WIDTH), w_out[:S5_WIDTH]), (o.reshape(t, -1), w_out[S5_WIDTH:])],
              ffn_norm[0, 1], wgu[0, 1], wdn[0, 1])

    xt = _ffn(xt, [], ffn_norm[1, 0], wgu[1, 0], wdn[1, 0])
    cos_t, sin_t = _rope_table(positions, DIFF_ROT, ROPE_THETA, 0, DIFF_HEAD_DIM)
    q, k, v = _odd_prep(xt, od_norm[0], od_w_in[0].astype(BF16), cos_t, sin_t)
    lam_init = 0.8 - 0.6 * math.exp(-0.3 * 1)
    lam = (jnp.exp(jnp.sum(diff_lq1[0] * diff_lk1[0])) - jnp.exp(jnp.sum(diff_lq2[0] * diff_lk2[0])) + lam_init)
    o = _diff_attn(lam.reshape(1, 1), q.reshape(nb, s, -1), k.reshape(nb, s, -1), v.reshape(nb, s, -1),
                   diff_subln[0], 1.0 - lam_init)
    xt = _ffn(xt, [(o.reshape(t, -1), od_w_out[0].astype(BF16))], ffn_norm[1, 1], wgu[1, 1], wdn[1, 1],
              gfinal=final_norm)
    return xt.reshape(nb, s, d)
```

```python
import functools
import math

import jax
import jax.numpy as jnp
from jax import lax
from jax.experimental import pallas as pl
from jax.experimental.pallas import tpu as pltpu

F32 = jnp.float32
BF16 = jnp.bfloat16

D_MODEL = 1024
NORM_EPS = 1e-6
D_FF = 2816
S5_WIDTH = 512
S5_GROUP_CH = 16
S5_GROUPS = 32
S5_STATE = 64
S5_FEATS = S5_GROUPS * S5_STATE
MLA_HEADS = 8
MLA_NOPE = 64
MLA_ROPE = 32
MLA_V = 64
MLA_Q_LORA = 384
MLA_KV_LORA = 256
MLA_ROPE_THETA = 10000.0
DIFF_HEADS = 8
DIFF_HEAD_DIM = 64
DIFF_ROT = 16
ROPE_THETA = 500000.0

LANES = 128
SUBLANES = 8
VMEM_LIMIT = 56 * 1024 * 1024
LOG2E = math.log2(math.e)

FFN_TM = 512
FFN_CHUNKS = 2
S5_TS = 256
S5_PITCH = S5_TS + SUBLANES
S5_ROWS = S5_FEATS // LANES
ATT_TQ = 512
ATT_TK = ATT_TQ // 2
PREP_TM = ATT_TQ

_NT = (((1,), (1,)), ((), ()))


def _rms(x, g):
    ms = jnp.mean(x * x, axis=-1, keepdims=True)
    return x * lax.rsqrt(ms + NORM_EPS) * g


def _const_spec(shape):
    nd = len(shape)
    return pl.BlockSpec(shape, lambda *_: (0,) * nd, pipeline_mode=pl.Buffered(1))


def _params(sem):
    return pltpu.CompilerParams(dimension_semantics=sem, vmem_limit_bytes=VMEM_LIMIT)


def _ffn_body(*refs, n_mix, final):
    x_ref = refs[0]
    mix_refs = refs[1:1 + 2 * n_mix]
    g_ref, wgu_ref, wd_ref = refs[1 + 2 * n_mix:4 + 2 * n_mix]
    gf_ref = refs[4 + 2 * n_mix] if final else None
    o_ref = refs[-1]

    x = x_ref[...]
    for i in range(n_mix):
        x = x + jnp.dot(mix_refs[2 * i][...], mix_refs[2 * i + 1][...], preferred_element_type=F32)
    h = _rms(x, g_ref[...]).astype(BF16)
    ck = D_FF // FFN_CHUNKS
    y = None
    for c in range(FFN_CHUNKS):
        g = jnp.dot(h, wgu_ref[:, c * ck:(c + 1) * ck], preferred_element_type=F32)
        u = jnp.dot(h, wgu_ref[:, D_FF + c * ck:D_FF + (c + 1) * ck], preferred_element_type=F32)
        a = (g * jax.nn.sigmoid(g) * u).astype(BF16)
        d = jnp.dot(a, wd_ref[c * ck:(c + 1) * ck, :], preferred_element_type=F32)
        y = d if y is None else y + d
    out = x + 0.5 * y
    if final:
        out = _rms(out, gf_ref[...])
    o_ref[...] = out


def _ffn(x, mixes, g, wgu, wd, gfinal=None):
    t = x.shape[0]
    n_mix = len(mixes)
    final = gfinal is not None
    in_specs = [pl.BlockSpec((FFN_TM, D_MODEL), lambda i: (i, 0))]
    args = [x]
    for m, w in mixes:
        in_specs += [pl.BlockSpec((FFN_TM, m.shape[1]), lambda i: (i, 0)), _const_spec(w.shape)]
        args += [m, w]
    in_specs += [_const_spec((1, D_MODEL)), _const_spec(wgu.shape), _const_spec(wd.shape)]
    args += [g.reshape(1, D_MODEL), wgu, wd]
    if final:
        in_specs.append(_const_spec((1, D_MODEL)))
        args.append(gfinal.reshape(1, D_MODEL))
    return pl.pallas_call(
        functools.partial(_ffn_body, n_mix=n_mix, final=final),
        grid=(t // FFN_TM,),
        in_specs=in_specs,
        out_specs=pl.BlockSpec((FFN_TM, D_MODEL), lambda i: (i, 0)),
        out_shape=jax.ShapeDtypeStruct((t, D_MODEL), F32),
        compiler_params=_params(("parallel",)),
        name="ffn",
    )(*args)


def _rope(blk, cos, sin, half, first_lo, period):
    lane = lax.broadcasted_iota(jnp.int32, blk.shape, 1) % period
    fwd = pltpu.roll(blk, LANES - half, 1)
    bwd = pltpu.roll(blk, half, 1)
    rot = jnp.where(lane < first_lo + half, fwd, bwd)
    return blk * cos + rot * sin


def _even_prep_body(x_ref, g_ref, win_ref, qn_ref, wuq_ref, kvn_ref, wuk_ref, wuvt_ref, cos_ref, sin_ref,
                    u_ref, q_ref, k_ref, vt_ref):
    h = _rms(x_ref[...], g_ref[...]).astype(BF16)
    proj = jnp.dot(h, win_ref[...], preferred_element_type=F32)
    o_cq = S5_WIDTH
    o_ckv = o_cq + MLA_Q_LORA
    o_kr = o_ckv + MLA_KV_LORA
    u_ref[...] = proj[:, :o_cq]
    cq = proj[:, o_cq:o_ckv]
    ckv = proj[:, o_ckv:o_kr]
    kr = proj[:, o_kr:o_kr + LANES]
    cos = cos_ref[...]
    sin = sin_ref[...]
    half = MLA_ROPE // 2
    scale = (MLA_NOPE + MLA_ROPE) ** -0.5 * LOG2E
    q = jnp.dot(_rms(cq, qn_ref[...]).astype(BF16), wuq_ref[...], preferred_element_type=F32)
    ckn = _rms(ckv, kvn_ref[...]).astype(BF16)
    kn = jnp.dot(ckn, wuk_ref[...], preferred_element_type=F32)
    vt_ref[0, 0] = lax.dot_general(wuvt_ref[...], ckn, _NT, preferred_element_type=F32).astype(BF16)
    kpe = _rope(kr, cos, sin, half, MLA_NOPE, LANES)
    for hd in range(MLA_HEADS):
        sl = slice(hd * LANES, (hd + 1) * LANES)
        q_ref[:, sl] = (_rope(q[:, sl], cos, sin, half, MLA_NOPE, LANES) * scale).astype(BF16)
        k_ref[:, sl] = (kn[:, sl] + kpe).astype(BF16)


def _vt_spec(nblk, rows):
    return pl.BlockSpec((1, 1, rows, PREP_TM), lambda i: (i // nblk, i % nblk, 0, 0))


def _even_prep(x, g, win, qn, wuq, kvn, wuk, wuvt, cos, sin, nb):
    t = x.shape[0]
    tm = PREP_TM
    nblk = t // nb // tm
    row = lambda w: pl.BlockSpec((tm, w), lambda i: (i, 0))
    return pl.pallas_call(
        _even_prep_body,
        grid=(t // tm,),
        in_specs=[row(D_MODEL), _const_spec((1, D_MODEL)), _const_spec(win.shape),
                  _const_spec((1, MLA_Q_LORA)), _const_spec(wuq.shape),
                  _const_spec((1, MLA_KV_LORA)), _const_spec(wuk.shape), _const_spec(wuvt.shape),
                  row(LANES), row(LANES)],
        out_specs=[row(S5_WIDTH), row(MLA_HEADS * LANES), row(MLA_HEADS * LANES),
                   _vt_spec(nblk, MLA_HEADS * MLA_V)],
        out_shape=[jax.ShapeDtypeStruct((t, S5_WIDTH), F32),
                   jax.ShapeDtypeStruct((t, MLA_HEADS * LANES), BF16),
                   jax.ShapeDtypeStruct((t, MLA_HEADS * LANES), BF16),
                   jax.ShapeDtypeStruct((nb, nblk, MLA_HEADS * MLA_V, tm), BF16)],
        compiler_params=_params(("parallel",)),
        name="even_prep",
    )(x, g.reshape(1, -1), win, qn.reshape(1, -1), wuq, kvn.reshape(1, -1), wuk, wuvt, cos, sin)


def _odd_prep_body(x_ref, g_ref, wqk_ref, wvt_ref, cos_ref, sin_ref, q_ref, k_ref, vt_ref):
    h = _rms(x_ref[...], g_ref[...]).astype(BF16)
    cos = cos_ref[...]
    sin = sin_ref[...]
    half = DIFF_ROT // 2
    width = DIFF_HEADS * 2 * DIFF_HEAD_DIM
    scale = DIFF_HEAD_DIM ** -0.5 * LOG2E
    q = jnp.dot(h, wqk_ref[:, :width], preferred_element_type=F32)
    k = jnp.dot(h, wqk_ref[:, width:], preferred_element_type=F32)
    vt_ref[0, 0] = lax.dot_general(wvt_ref[...], h, _NT, preferred_element_type=F32).astype(BF16)
    for c in range(width // LANES):
        sl = slice(c * LANES, (c + 1) * LANES)
        q_ref[:, sl] = (_rope(q[:, sl], cos, sin, half, 0, DIFF_HEAD_DIM) * scale).astype(BF16)
        k_ref[:, sl] = _rope(k[:, sl], cos, sin, half, 0, DIFF_HEAD_DIM).astype(BF16)


def _odd_prep(x, g, wqk, wvt, cos, sin, nb):
    t = x.shape[0]
    tm = PREP_TM
    nblk = t // nb // tm
    width = DIFF_HEADS * 2 * DIFF_HEAD_DIM
    row = lambda w: pl.BlockSpec((tm, w), lambda i: (i, 0))
    return pl.pallas_call(
        _odd_prep_body,
        grid=(t // tm,),
        in_specs=[row(D_MODEL), _const_spec((1, D_MODEL)), _const_spec(wqk.shape), _const_spec(wvt.shape),
                  row(LANES), row(LANES)],
        out_specs=[row(width), row(width), _vt_spec(nblk, width)],
        out_shape=[jax.ShapeDtypeStruct((t, width), BF16), jax.ShapeDtypeStruct((t, width), BF16),
                   jax.ShapeDtypeStruct((nb, nblk, width, tm), BF16)],
        compiler_params=_params(("parallel",)),
        name="odd_prep",
    )(x, g.reshape(1, -1), wqk, wvt, cos, sin)


def _s5_body(u_ref, ar_ref, ai_ref, bre_ref, bim_ref, c_ref, d_ref, wglu_ref, bglu_ref, o_ref,
             xr_ref, xi_ref, sr_ref, si_ref):
    nb = u_ref.shape[0]

    @pl.when(pl.program_id(0) == 0)
    def _():
        sr_ref[...] = jnp.zeros_like(sr_ref)
        si_ref[...] = jnp.zeros_like(si_ref)

    for b in range(nb):
        ub = u_ref[b].astype(BF16)
        bur = jnp.dot(ub, bre_ref[...], preferred_element_type=F32)
        bui = jnp.dot(ub, bim_ref[...], preferred_element_type=F32)
        for j in range(S5_ROWS):
            xr_ref[b, j * S5_PITCH:j * S5_PITCH + S5_TS, :] = bur[:, j * LANES:(j + 1) * LANES]
            xi_ref[b, j * S5_PITCH:j * S5_PITCH + S5_TS, :] = bui[:, j * LANES:(j + 1) * LANES]

    ar = ar_ref[...]
    ai = ai_ref[...]

    def step(t, carry):
        new = []
        for b in range(nb):
            pr, pi = carry[2 * b], carry[2 * b + 1]
            idx = pl.ds(t, S5_ROWS, stride=S5_PITCH)
            nr = ar * pr - ai * pi + xr_ref[b, idx, :]
            ni = ar * pi + ai * pr + xi_ref[b, idx, :]
            xr_ref[b, idx, :] = nr
            xi_ref[b, idx, :] = ni
            new += [nr, ni]
        return tuple(new)

    init = []
    for b in range(nb):
        init += [sr_ref[b], si_ref[b]]
    fin = lax.fori_loop(0, S5_TS, step, tuple(init), unroll=4)
    for b in range(nb):
        sr_ref[b] = fin[2 * b]
        si_ref[b] = fin[2 * b + 1]

    for b in range(nb):
        parts = [xr_ref[b, j * S5_PITCH:j * S5_PITCH + S5_TS, :].astype(BF16) for j in range(S5_ROWS)]
        parts += [xi_ref[b, j * S5_PITCH:j * S5_PITCH + S5_TS, :].astype(BF16) for j in range(S5_ROWS)]
        xs = jnp.concatenate(parts, axis=1)
        y = jnp.dot(xs, c_ref[...], preferred_element_type=F32) + d_ref[...] * u_ref[b]
        g = 0.5 * y * (1.0 + jnp.tanh(math.sqrt(2.0 / math.pi) * (y + 0.044715 * (y * y * y))))
        z = jnp.dot(g.astype(BF16), wglu_ref[...], preferred_element_type=F32) + bglu_ref[...]
        o_ref[b] = (g * jax.nn.sigmoid(z)).astype(BF16)


def _s5(u, ar, ai, bre, bim, cmat, dskip, wglu, bglu):
    nb, s, _ = u.shape
    blk = pl.BlockSpec((nb, S5_TS, S5_WIDTH), lambda i: (0, i, 0))
    return pl.pallas_call(
        _s5_body,
        grid=(s // S5_TS,),
        in_specs=[blk, _const_spec(ar.shape), _const_spec(ai.shape), _const_spec(bre.shape),
                  _const_spec(bim.shape), _const_spec(cmat.shape), _const_spec((1, S5_WIDTH)),
                  _const_spec(wglu.shape), _const_spec((1, S5_WIDTH))],
        out_specs=blk,
        out_shape=jax.ShapeDtypeStruct((nb, s, S5_WIDTH), BF16),
        scratch_shapes=[pltpu.VMEM((nb, S5_ROWS * S5_PITCH, LANES), F32),
                        pltpu.VMEM((nb, S5_ROWS * S5_PITCH, LANES), F32),
                        pltpu.VMEM((nb, S5_ROWS, LANES), F32),
                        pltpu.VMEM((nb, S5_ROWS, LANES), F32)],
        compiler_params=_params(("arbitrary",)),
        name="s5",
    )(u, ar, ai, bre, bim, cmat, dskip.reshape(1, -1), wglu, bglu.reshape(1, -1))


def _flash_t(score_fn, vt_ref, qi, scr):
    s0, s1, p0, p1, a0, a1, m_ref, l_ref, acc_ref = scr
    m_ref[...] = jnp.full(m_ref.shape, -jnp.inf, F32)
    l_ref[...] = jnp.zeros(l_ref.shape, F32)
    acc_ref[...] = jnp.zeros(acc_ref.shape, F32)
    p1[...] = jnp.zeros(p1.shape, BF16)
    a1[...] = jnp.ones(a1.shape, F32)

    def softmax(s_ref, p_ref, a_ref, key_off):
        s = s_ref[...]
        if key_off is not None:
            kpos = lax.broadcasted_iota(jnp.int32, s.shape, 0) + key_off
            qpos = lax.broadcasted_iota(jnp.int32, s.shape, 1) % ATT_TQ
            s = jnp.where(kpos <= qpos, s, -jnp.inf)
        m_old = m_ref[...]
        m_new = jnp.maximum(m_old, jnp.max(s, axis=0, keepdims=True))
        alpha = jnp.exp2(m_old - m_new)
        p = jnp.exp2(s - m_new)
        l_ref[...] = alpha * l_ref[...] + jnp.sum(p, axis=0, keepdims=True)
        m_ref[...] = m_new
        a_ref[...] = alpha
        p_ref[...] = p.astype(BF16)

    def value(pair, half, p_ref, a_ref):
        vt = vt_ref[0, pair, :, half * ATT_TK:(half + 1) * ATT_TK]
        acc_ref[...] = a_ref[...] * acc_ref[...] + jnp.dot(vt, p_ref[...], preferred_element_type=F32)

    s0[...] = score_fn(0)

    def pair_step(i, carry):
        s1[...] = score_fn(2 * i + 1)
        softmax(s0, p0, a0, None)
        value(jnp.maximum(i - 1, 0), 1, p1, a1)
        s0[...] = score_fn(2 * i + 2)
        softmax(s1, p1, a1, None)
        value(i, 0, p0, a0)
        return carry

    lax.fori_loop(0, qi, pair_step, 0)
    s1[...] = score_fn(2 * qi + 1)
    softmax(s0, p0, a0, 0)
    value(jnp.maximum(qi - 1, 0), 1, p1, a1)
    softmax(s1, p1, a1, ATT_TK)
    value(qi, 0, p0, a0)
    value(qi, 1, p1, a1)
    return acc_ref[...] / l_ref[...]


def _mla_attn_body(q_ref, k_ref, vt_ref, o_ref, *scr):
    qi = pl.program_id(2)
    qt_ref = scr[0]
    qt_ref[:, :ATT_TQ] = q_ref[0, :, :LANES].astype(F32).T.astype(BF16)
    qt_ref[:, ATT_TQ:] = q_ref[0, :, LANES:].astype(F32).T.astype(BF16)

    def scores(j):
        kb = k_ref[0, pl.ds(j * ATT_TK, ATT_TK), :]
        sa = jnp.dot(kb[:, :LANES], qt_ref[:, :ATT_TQ], preferred_element_type=F32)
        sb = jnp.dot(kb[:, LANES:], qt_ref[:, ATT_TQ:], preferred_element_type=F32)
        return jnp.concatenate([sa, sb], axis=1)

    ot = _flash_t(scores, vt_ref, qi, scr[1:])
    row = lax.broadcasted_iota(jnp.int32, (LANES, ATT_TQ), 0)
    o_ref[0] = jnp.where(row < MLA_V, ot[:, :ATT_TQ], ot[:, ATT_TQ:]).T.astype(BF16)


def _attn_specs(s, qk_lanes):
    npair = s // ATT_TQ
    n = 2 * ATT_TQ
    row = pltpu.VMEM((1, n), F32)
    return dict(
        in_specs=[pl.BlockSpec((1, ATT_TQ, qk_lanes), lambda b, h, i: (b, i, h)),
                  pl.BlockSpec((1, s, qk_lanes), lambda b, h, i: (b, 0, h)),
                  pl.BlockSpec((1, npair, LANES, ATT_TQ), lambda b, h, i: (b, 0, h, 0))],
        out_specs=pl.BlockSpec((1, ATT_TQ, LANES), lambda b, h, i: (b, i, h)),
        scratch_shapes=[pltpu.VMEM((LANES, n), BF16),
                        pltpu.VMEM((ATT_TK, n), F32), pltpu.VMEM((ATT_TK, n), F32),
                        pltpu.VMEM((ATT_TK, n), BF16), pltpu.VMEM((ATT_TK, n), BF16),
                        row, row, row, row, pltpu.VMEM((LANES, n), F32)],
        compiler_params=_params(("parallel", "parallel", "arbitrary")))


def _mla_attn(q, k, vt):
    nb, s, _ = q.shape
    return pl.pallas_call(
        _mla_attn_body,
        grid=(nb, MLA_HEADS // 2, s // ATT_TQ),
        out_shape=jax.ShapeDtypeStruct((nb, s, MLA_HEADS * MLA_V), BF16),
        name="mla_attn", **_attn_specs(s, 2 * LANES),
    )(q, k, vt)


def _diff_attn_body(lam_ref, q_ref, k_ref, vt_ref, sub_ref, o_ref, *scr, out_scale):
    qi = pl.program_id(2)
    qt_ref = scr[0]
    qt = q_ref[0].astype(F32).T
    row = lax.broadcasted_iota(jnp.int32, qt.shape, 0)
    zero = jnp.zeros_like(qt)
    qt_ref[:, :ATT_TQ] = jnp.where(row < DIFF_HEAD_DIM, qt, zero).astype(BF16)
    qt_ref[:, ATT_TQ:] = jnp.where(row < DIFF_HEAD_DIM, zero, qt).astype(BF16)

    def scores(j):
        return jnp.dot(k_ref[0, pl.ds(j * ATT_TK, ATT_TK), :], qt_ref[...], preferred_element_type=F32)

    ot = _flash_t(scores, vt_ref, qi, scr[1:])
    o = (ot[:, :ATT_TQ] - lam_ref[0, 0] * ot[:, ATT_TQ:]).T
    o_ref[0] = (_rms(o, sub_ref[...]) * out_scale).astype(BF16)


def _diff_attn(lam, q, k, vt, subln, out_scale):
    nb, s, _ = q.shape
    spec = _attn_specs(s, LANES)
    spec["in_specs"] = ([pl.BlockSpec(memory_space=pltpu.SMEM)] + spec["in_specs"]
                        + [pl.BlockSpec((1, LANES), lambda b, h, i: (0, 0))])
    return pl.pallas_call(
        functools.partial(_diff_attn_body, out_scale=out_scale),
        grid=(nb, DIFF_HEADS, s // ATT_TQ),
        out_shape=jax.ShapeDtypeStruct((nb, s, DIFF_HEADS * 2 * DIFF_HEAD_DIM), BF16),
        name="diff_attn", **spec,
    )(lam, q, k, vt, subln.reshape(1, -1))


def _rope_table(positions, rot_dim, theta, first_lo, period):
    half = rot_dim // 2
    inv = theta ** (-jnp.arange(half, dtype=F32) * 2.0 / rot_dim)
    ang = positions.astype(F32).reshape(-1, 1) * inv
    cos, sin = jnp.cos(ang), jnp.sin(ang)
    t = ang.shape[0]
    pad_lo = jnp.zeros((t, first_lo), F32)
    pad_hi = jnp.zeros((t, period - first_lo - rot_dim), F32)
    cos_p = jnp.concatenate([pad_lo + 1.0, cos, cos, pad_hi + 1.0], axis=1)
    sin_p = jnp.concatenate([pad_lo, -sin, sin, pad_hi], axis=1)
    reps = LANES // period
    return jnp.tile(cos_p, (1, reps)), jnp.tile(sin_p, (1, reps))


def _s5_discretize(lam_re, lam_im, log_dt, b_re, b_im):
    dt = jnp.exp(log_dt)[:, None]
    mag = jnp.exp(lam_re * dt)
    ang = lam_im * dt
    ab_re = mag * jnp.cos(ang)
    ab_im = mag * jnp.sin(ang)
    den = lam_re * lam_re + lam_im * lam_im
    nr = ab_re - 1.0
    f_re = (nr * lam_re + ab_im * lam_im) / den
    f_im = (ab_im * lam_re - nr * lam_im) / den
    bb_re = f_re[..., None] * b_re - f_im[..., None] * b_im
    bb_im = f_re[..., None] * b_im + f_im[..., None] * b_re
    return ab_re, ab_im, bb_re, bb_im


def _block_diag_in(bb):
    eye = jnp.eye(S5_GROUPS, dtype=F32)
    return jnp.einsum('gph,gk->ghkp', bb, eye).reshape(S5_WIDTH, S5_FEATS)


def _block_diag_out(c):
    eye = jnp.eye(S5_GROUPS, dtype=F32)
    return jnp.einsum('ghp,gk->gpkh', c, eye).reshape(S5_FEATS, S5_WIDTH)


def _pad_heads(w, heads, width, lo, hi):
    kdim = w.shape[0]
    w = w.reshape(kdim, heads, width)[:, :, lo:hi]
    w = jnp.pad(w, ((0, 0), (0, 0), (0, LANES - (hi - lo))))
    return w.reshape(kdim, heads * LANES)


def kernel(x, positions, ffn_norm, ffn_w_gu, ffn_w_down, ev_norm, ev_w_in, s5_lambda_re, s5_lambda_im, s5_log_dt, s5_b_re, s5_b_im, s5_c_re, s5_c_im, s5_d, s5_w_glu, s5_b_glu, mla_q_norm, mla_w_uq, mla_kv_norm, mla_w_ukv, ev_w_out, od_norm, od_w_in, diff_lq1, diff_lk1, diff_lq2, diff_lk2, diff_subln, od_w_out, final_norm):
    nb, s, d = x.shape
    t = nb * s
    xt = x.reshape(t, d)
    wgu = ffn_w_gu.astype(BF16)
    wdn = ffn_w_down.astype(BF16)

    xt = _ffn(xt, [], ffn_norm[0, 0], wgu[0, 0], wdn[0, 0])

    cos_m, sin_m = _rope_table(positions, MLA_ROPE, MLA_ROPE_THETA, MLA_NOPE, LANES)
    w_in = ev_w_in[0]
    o_kr = S5_WIDTH + MLA_Q_LORA + MLA_KV_LORA
    w_kr = jnp.pad(w_in[:, o_kr:], ((0, 0), (MLA_NOPE, LANES - MLA_NOPE - MLA_ROPE)))
    win_p = jnp.concatenate([w_in[:, :o_kr], w_kr], axis=1).astype(BF16)
    wuq_p = _pad_heads(mla_w_uq[0], MLA_HEADS, MLA_NOPE + MLA_ROPE, 0, MLA_NOPE + MLA_ROPE).astype(BF16)
    wuk_p = _pad_heads(mla_w_ukv[0], MLA_HEADS, MLA_NOPE + MLA_V, 0, MLA_NOPE).astype(BF16)
    wuv = mla_w_ukv[0].reshape(MLA_KV_LORA, MLA_HEADS, MLA_NOPE + MLA_V)[:, :, MLA_NOPE:]
    wuvt = wuv.reshape(MLA_KV_LORA, MLA_HEADS * MLA_V).T.astype(BF16)
    u, q, k, vt = _even_prep(xt, ev_norm[0], win_p, mla_q_norm[0], wuq_p, mla_kv_norm[0], wuk_p, wuvt,
                             cos_m, sin_m, nb)

    ab_re, ab_im, bb_re, bb_im = _s5_discretize(s5_lambda_re[0], s5_lambda_im[0], s5_log_dt[0], s5_b_re[0], s5_b_im[0])
    cmat = jnp.concatenate([_block_diag_out(s5_c_re[0]), -_block_diag_out(s5_c_im[0])], axis=0).astype(BF16)
    s5_out = _s5(u.reshape(nb, s, S5_WIDTH), ab_re.reshape(S5_ROWS, LANES), ab_im.reshape(S5_ROWS, LANES),
                 _block_diag_in(bb_re).astype(BF16), _block_diag_in(bb_im).astype(BF16), cmat,
                 s5_d[0].reshape(-1), s5_w_glu[0].astype(BF16), s5_b_glu[0])

    o = _mla_attn(q.reshape(nb, s, -1), k.reshape(nb, s, -1), vt)
    w_out = ev_w_out[0].astype(BF16)
    xt = _ffn(xt, [(s5_out.reshape(t, S5_WIDTH), w_out[:S5_WIDTH]), (o.reshape(t, -1), w_out[S5_WIDTH:])],
              ffn_norm[0, 1], wgu[0, 1], wdn[0, 1])

    xt = _ffn(xt, [], ffn_norm[1, 0], wgu[1, 0], wdn[1, 0])
    cos_t, sin_t = _rope_table(positions, DIFF_ROT, ROPE_THETA, 0, DIFF_HEAD_DIM)
    width = DIFF_HEADS * 2 * DIFF_HEAD_DIM
    w_in = od_w_in[0]
    q, k, vt = _odd_prep(xt, od_norm[0], w_in[:, :2 * width].astype(BF16), w_in[:, 2 * width:].T.astype(BF16),
                         cos_t, sin_t, nb)
    lam_init = 0.8 - 0.6 * math.exp(-0.3 * 1)
    lam = (jnp.exp(jnp.sum(diff_lq1[0] * diff_lk1[0])) - jnp.exp(jnp.sum(diff_lq2[0] * diff_lk2[0])) + lam_init)
    o = _diff_attn(lam.reshape(1, 1), q.reshape(nb, s, -1), k.reshape(nb, s, -1), vt, diff_subln[0],
                   1.0 - lam_init)
    xt = _ffn(xt, [(o.reshape(t, -1), od_w_out[0].astype(BF16))], ffn_norm[1, 1], wgu[1, 1], wdn[1, 1],
              gfinal=final_norm)
    return xt.reshape(nb, s, d)
```

```python
import functools
import math

import jax
import jax.numpy as jnp
from jax import lax
from jax.experimental import pallas as pl
from jax.experimental.pallas import tpu as pltpu

F32 = jnp.float32
BF16 = jnp.bfloat16

D_MODEL = 1024
NORM_EPS = 1e-6
D_FF = 2816
S5_WIDTH = 512
S5_GROUP_CH = 16
S5_GROUPS = 32
S5_STATE = 64
S5_FEATS = S5_GROUPS * S5_STATE
MLA_HEADS = 8
MLA_NOPE = 64
MLA_ROPE = 32
MLA_V = 64
MLA_Q_LORA = 384
MLA_KV_LORA = 256
MLA_ROPE_THETA = 10000.0
DIFF_HEADS = 8
DIFF_HEAD_DIM = 64
DIFF_ROT = 16
ROPE_THETA = 500000.0

LANES = 128
SUBLANES = 8
VMEM_LIMIT = 56 * 1024 * 1024
LOG2E = math.log2(math.e)

FFN_TM = 512
FFN_CHUNKS = 2
S5_TS = 256
S5_PITCH = S5_TS + SUBLANES
S5_ROWS = S5_FEATS // LANES
ATT_TQ = 512
ATT_TK = ATT_TQ // 2
ATT_LANES = 4 * ATT_TQ
ATT_STRIP = ATT_TK
ATT_SUM_ROWS = 16
PREP_TM = ATT_TQ

_NT = (((1,), (1,)), ((), ()))


def _rms(x, g):
    ms = jnp.mean(x * x, axis=-1, keepdims=True)
    return x * lax.rsqrt(ms + NORM_EPS) * g


def _const_spec(shape):
    nd = len(shape)
    return pl.BlockSpec(shape, lambda *_: (0,) * nd, pipeline_mode=pl.Buffered(1))


def _params(sem, flags=None):
    return pltpu.CompilerParams(dimension_semantics=sem, vmem_limit_bytes=VMEM_LIMIT, flags=flags)


def _ffn_body(*refs, n_mix, final):
    x_ref = refs[0]
    mix_refs = refs[1:1 + 2 * n_mix]
    g_ref, wgu_ref, wd_ref = refs[1 + 2 * n_mix:4 + 2 * n_mix]
    gf_ref = refs[4 + 2 * n_mix] if final else None
    o_ref = refs[-1]

    x = x_ref[...]
    for i in range(n_mix):
        x = x + jnp.dot(mix_refs[2 * i][...], mix_refs[2 * i + 1][...], preferred_element_type=F32)
    h = _rms(x, g_ref[...]).astype(BF16)
    ck = D_FF // FFN_CHUNKS
    y = None
    for c in range(FFN_CHUNKS):
        g = jnp.dot(h, wgu_ref[:, c * ck:(c + 1) * ck], preferred_element_type=F32)
        u = jnp.dot(h, wgu_ref[:, D_FF + c * ck:D_FF + (c + 1) * ck], preferred_element_type=F32)
        a = (g * jax.nn.sigmoid(g) * u).astype(BF16)
        d = jnp.dot(a, wd_ref[c * ck:(c + 1) * ck, :], preferred_element_type=F32)
        y = d if y is None else y + d
    out = x + 0.5 * y
    if final:
        out = _rms(out, gf_ref[...])
    o_ref[...] = out


def _ffn(x, mixes, g, wgu, wd, gfinal=None):
    t = x.shape[0]
    n_mix = len(mixes)
    final = gfinal is not None
    in_specs = [pl.BlockSpec((FFN_TM, D_MODEL), lambda i: (i, 0))]
    args = [x]
    for m, w in mixes:
        in_specs += [pl.BlockSpec((FFN_TM, m.shape[1]), lambda i: (i, 0)), _const_spec(w.shape)]
        args += [m, w]
    in_specs += [_const_spec((1, D_MODEL)), _const_spec(wgu.shape), _const_spec(wd.shape)]
    args += [g.reshape(1, D_MODEL), wgu, wd]
    if final:
        in_specs.append(_const_spec((1, D_MODEL)))
        args.append(gfinal.reshape(1, D_MODEL))
    return pl.pallas_call(
        functools.partial(_ffn_body, n_mix=n_mix, final=final),
        grid=(t // FFN_TM,),
        in_specs=in_specs,
        out_specs=pl.BlockSpec((FFN_TM, D_MODEL), lambda i: (i, 0)),
        out_shape=jax.ShapeDtypeStruct((t, D_MODEL), F32),
        compiler_params=_params(("parallel",)),
        name="ffn",
    )(*args)


def _rope(blk, cos, sin, half, first_lo, period):
    lane = lax.broadcasted_iota(jnp.int32, blk.shape, 1) % period
    fwd = pltpu.roll(blk, LANES - half, 1)
    bwd = pltpu.roll(blk, half, 1)
    rot = jnp.where(lane < first_lo + half, fwd, bwd)
    return blk * cos + rot * sin


def _even_prep_body(x_ref, g_ref, win_ref, qn_ref, wuq_ref, kvn_ref, wuk_ref, wuvt_ref, cos_ref, sin_ref,
                    u_ref, q_ref, k_ref, vt_ref):
    h = _rms(x_ref[...], g_ref[...]).astype(BF16)
    proj = jnp.dot(h, win_ref[...], preferred_element_type=F32)
    o_cq = S5_WIDTH
    o_ckv = o_cq + MLA_Q_LORA
    o_kr = o_ckv + MLA_KV_LORA
    u_ref[...] = proj[:, :o_cq]
    cq = proj[:, o_cq:o_ckv]
    ckv = proj[:, o_ckv:o_kr]
    kr = proj[:, o_kr:o_kr + LANES]
    cos = cos_ref[...]
    sin = sin_ref[...]
    half = MLA_ROPE // 2
    scale = (MLA_NOPE + MLA_ROPE) ** -0.5 * LOG2E
    q = jnp.dot(_rms(cq, qn_ref[...]).astype(BF16), wuq_ref[...], preferred_element_type=F32)
    ckn = _rms(ckv, kvn_ref[...]).astype(BF16)
    kn = jnp.dot(ckn, wuk_ref[...], preferred_element_type=F32)
    vt_ref[0, 0] = lax.dot_general(wuvt_ref[...], ckn, _NT, preferred_element_type=F32).astype(BF16)
    kpe = _rope(kr, cos, sin, half, MLA_NOPE, LANES)
    for hd in range(MLA_HEADS):
        sl = slice(hd * LANES, (hd + 1) * LANES)
        q_ref[:, sl] = (_rope(q[:, sl], cos, sin, half, MLA_NOPE, LANES) * scale).astype(BF16)
        k_ref[:, sl] = (kn[:, sl] + kpe).astype(BF16)


def _vt_spec(nblk, rows):
    return pl.BlockSpec((1, 1, rows, PREP_TM), lambda i: (i // nblk, i % nblk, 0, 0))


def _even_prep(x, g, win, qn, wuq, kvn, wuk, wuvt, cos, sin, nb):
    t = x.shape[0]
    tm = PREP_TM
    nblk = t // nb // tm
    row = lambda w: pl.BlockSpec((tm, w), lambda i: (i, 0))
    return pl.pallas_call(
        _even_prep_body,
        grid=(t // tm,),
        in_specs=[row(D_MODEL), _const_spec((1, D_MODEL)), _const_spec(win.shape),
                  _const_spec((1, MLA_Q_LORA)), _const_spec(wuq.shape),
                  _const_spec((1, MLA_KV_LORA)), _const_spec(wuk.shape), _const_spec(wuvt.shape),
                  row(LANES), row(LANES)],
        out_specs=[row(S5_WIDTH), row(MLA_HEADS * LANES), row(MLA_HEADS * LANES),
                   _vt_spec(nblk, MLA_HEADS * MLA_V)],
        out_shape=[jax.ShapeDtypeStruct((t, S5_WIDTH), F32),
                   jax.ShapeDtypeStruct((t, MLA_HEADS * LANES), BF16),
                   jax.ShapeDtypeStruct((t, MLA_HEADS * LANES), BF16),
                   jax.ShapeDtypeStruct((nb, nblk, MLA_HEADS * MLA_V, tm), BF16)],
        compiler_params=_params(("parallel",)),
        name="even_prep",
    )(x, g.reshape(1, -1), win, qn.reshape(1, -1), wuq, kvn.reshape(1, -1), wuk, wuvt, cos, sin)


def _odd_prep_body(x_ref, g_ref, wqk_ref, wvt_ref, cos_ref, sin_ref, q_ref, k_ref, vt_ref):
    h = _rms(x_ref[...], g_ref[...]).astype(BF16)
    cos = cos_ref[...]
    sin = sin_ref[...]
    half = DIFF_ROT // 2
    width = DIFF_HEADS * 2 * DIFF_HEAD_DIM
    scale = DIFF_HEAD_DIM ** -0.5 * LOG2E
    q = jnp.dot(h, wqk_ref[:, :width], preferred_element_type=F32)
    k = jnp.dot(h, wqk_ref[:, width:], preferred_element_type=F32)
    vt_ref[0, 0] = lax.dot_general(wvt_ref[...], h, _NT, preferred_element_type=F32).astype(BF16)
    for c in range(width // LANES):
        sl = slice(c * LANES, (c + 1) * LANES)
        q_ref[:, sl] = (_rope(q[:, sl], cos, sin, half, 0, DIFF_HEAD_DIM) * scale).astype(BF16)
        k_ref[:, sl] = _rope(k[:, sl], cos, sin, half, 0, DIFF_HEAD_DIM).astype(BF16)


def _odd_prep(x, g, wqk, wvt, cos, sin, nb):
    t = x.shape[0]
    tm = PREP_TM
    nblk = t // nb // tm
    width = DIFF_HEADS * 2 * DIFF_HEAD_DIM
    row = lambda w: pl.BlockSpec((tm, w), lambda i: (i, 0))
    return pl.pallas_call(
        _odd_prep_body,
        grid=(t // tm,),
        in_specs=[row(D_MODEL), _const_spec((1, D_MODEL)), _const_spec(wqk.shape), _const_spec(wvt.shape),
                  row(LANES), row(LANES)],
        out_specs=[row(width), row(width), _vt_spec(nblk, width)],
        out_shape=[jax.ShapeDtypeStruct((t, width), BF16), jax.ShapeDtypeStruct((t, width), BF16),
                   jax.ShapeDtypeStruct((nb, nblk, width, tm), BF16)],
        compiler_params=_params(("parallel",)),
        name="odd_prep",
    )(x, g.reshape(1, -1), wqk, wvt, cos, sin)


def _s5_body(u_ref, ar_ref, ai_ref, bre_ref, bim_ref, c_ref, d_ref, wglu_ref, bglu_ref, o_ref,
             xr_ref, xi_ref, sr_ref, si_ref):
    nb = u_ref.shape[0]

    @pl.when(pl.program_id(0) == 0)
    def _():
        sr_ref[...] = jnp.zeros_like(sr_ref)
        si_ref[...] = jnp.zeros_like(si_ref)

    for b in range(nb):
        ub = u_ref[b].astype(BF16)
        bur = jnp.dot(ub, bre_ref[...], preferred_element_type=F32)
        bui = jnp.dot(ub, bim_ref[...], preferred_element_type=F32)
        for j in range(S5_ROWS):
            xr_ref[b, j * S5_PITCH:j * S5_PITCH + S5_TS, :] = bur[:, j * LANES:(j + 1) * LANES]
            xi_ref[b, j * S5_PITCH:j * S5_PITCH + S5_TS, :] = bui[:, j * LANES:(j + 1) * LANES]

    ar = ar_ref[...]
    ai = ai_ref[...]

    def step(t, carry):
        new = []
        for b in range(nb):
            pr, pi = carry[2 * b], carry[2 * b + 1]
            idx = pl.ds(t, S5_ROWS, stride=S5_PITCH)
            nr = ar * pr - ai * pi + xr_ref[b, idx, :]
            ni = ar * pi + ai * pr + xi_ref[b, idx, :]
            xr_ref[b, idx, :] = nr
            xi_ref[b, idx, :] = ni
            new += [nr, ni]
        return tuple(new)

    init = []
    for b in range(nb):
        init += [sr_ref[b], si_ref[b]]
    fin = lax.fori_loop(0, S5_TS, step, tuple(init), unroll=4)
    for b in range(nb):
        sr_ref[b] = fin[2 * b]
        si_ref[b] = fin[2 * b + 1]

    for b in range(nb):
        parts = [xr_ref[b, j * S5_PITCH:j * S5_PITCH + S5_TS, :].astype(BF16) for j in range(S5_ROWS)]
        parts += [xi_ref[b, j * S5_PITCH:j * S5_PITCH + S5_TS, :].astype(BF16) for j in range(S5_ROWS)]
        xs = jnp.concatenate(parts, axis=1)
        y = jnp.dot(xs, c_ref[...], preferred_element_type=F32) + d_ref[...] * u_ref[b]
        g = 0.5 * y * (1.0 + jnp.tanh(math.sqrt(2.0 / math.pi) * (y + 0.044715 * (y * y * y))))
        z = jnp.dot(g.astype(BF16), wglu_ref[...], preferred_element_type=F32) + bglu_ref[...]
        o_ref[b] = (g * jax.nn.sigmoid(z)).astype(BF16)


def _s5(u, ar, ai, bre, bim, cmat, dskip, wglu, bglu):
    nb, s, _ = u.shape
    blk = pl.BlockSpec((nb, S5_TS, S5_WIDTH), lambda i: (0, i, 0))
    return pl.pallas_call(
        _s5_body,
        grid=(s // S5_TS,),
        in_specs=[blk, _const_spec(ar.shape), _const_spec(ai.shape), _const_spec(bre.shape),
                  _const_spec(bim.shape), _const_spec(cmat.shape), _const_spec((1, S5_WIDTH)),
                  _const_spec(wglu.shape), _const_spec((1, S5_WIDTH))],
        out_specs=blk,
        out_shape=jax.ShapeDtypeStruct((nb, s, S5_WIDTH), BF16),
        scratch_shapes=[pltpu.VMEM((nb, S5_ROWS * S5_PITCH, LANES), F32),
                        pltpu.VMEM((nb, S5_ROWS * S5_PITCH, LANES), F32),
                        pltpu.VMEM((nb, S5_ROWS, LANES), F32),
                        pltpu.VMEM((nb, S5_ROWS, LANES), F32)],
        compiler_params=_params(("arbitrary",)),
        name="s5",
    )(u, ar, ai, bre, bim, cmat, dskip.reshape(1, -1), wglu, bglu.reshape(1, -1))


def _flash_t(k_ref, vt_ref, qt_ref, qi, scr, k_block, v_block):
    s0, s1, p0, p1, a0, a1, m_ref, acc_ref = scr
    strips = tuple(range(ATT_LANES // ATT_STRIP))
    lower = tuple(c for c in strips if (c * ATT_STRIP) % ATT_TQ == 0)
    upper = tuple(c for c in strips if (c * ATT_STRIP) % ATT_TQ != 0)
    m_ref[...] = jnp.full(m_ref.shape, -jnp.inf, F32)
    acc_ref[...] = jnp.zeros(acc_ref.shape, F32)
    p1[...] = jnp.zeros(p1.shape, BF16)
    a1[...] = jnp.ones(a1.shape, F32)
    ones = jnp.ones((ATT_SUM_ROWS, ATT_TK), BF16)

    def scores(j, s_ref, which):
        for c in which:
            sl = slice(c * ATT_STRIP, (c + 1) * ATT_STRIP)
            kb = k_ref[0, pl.ds(j * ATT_TK, ATT_TK), k_block(c) * LANES:(k_block(c) + 1) * LANES]
            s_ref[:, sl] = jnp.dot(kb, qt_ref[:, sl], preferred_element_type=F32)

    def softmax(s_ref, p_ref, a_ref, which, diagonal):
        for c in which:
            for t in range(ATT_STRIP // LANES):
                sl = slice(c * ATT_STRIP + t * LANES, c * ATT_STRIP + (t + 1) * LANES)
                s = s_ref[:, sl]
                if diagonal:
                    kpos = lax.broadcasted_iota(jnp.int32, s.shape, 0)
                    qpos = lax.broadcasted_iota(jnp.int32, s.shape, 1) + t * LANES
                    s = jnp.where(kpos <= qpos, s, -jnp.inf)
                m_old = m_ref[:, sl]
                m_new = jnp.maximum(m_old, jnp.max(s, axis=0, keepdims=True))
                a_ref[:, sl] = jnp.exp2(m_old - m_new)
                m_ref[:, sl] = m_new
                p_ref[:, sl] = jnp.exp2(s - m_new).astype(BF16)

    def value(pair, half, p_ref, a_ref, which):
        for c in which:
            sl = slice(c * ATT_STRIP, (c + 1) * ATT_STRIP)
            vt = vt_ref[0, pair, v_block(c) * LANES:(v_block(c) + 1) * LANES, half * ATT_TK:(half + 1) * ATT_TK]
            pv = jnp.dot(jnp.concatenate([vt, ones], axis=0), p_ref[:, sl], preferred_element_type=F32)
            acc_ref[:, sl] = a_ref[:, sl] * acc_ref[:, sl] + pv

    scores(0, s0, strips)

    def pair_step(i, carry):
        softmax(s0, p0, a0, strips, False)
        scores(2 * i + 1, s1, strips)
        value(jnp.maximum(i - 1, 0), 1, p1, a1, strips)
        softmax(s1, p1, a1, strips, False)
        scores(2 * i + 2, s0, strips)
        value(i, 0, p0, a0, strips)
        return carry

    lax.fori_loop(0, qi, pair_step, 0)
    scores(2 * qi + 1, s1, upper)
    softmax(s0, p0, a0, lower, True)
    softmax(s0, p0, a0, upper, False)
    value(jnp.maximum(qi - 1, 0), 1, p1, a1, strips)
    softmax(s1, p1, a1, upper, True)
    value(qi, 0, p0, a0, strips)
    value(qi, 1, p1, a1, upper)
    return acc_ref[:LANES] / acc_ref[LANES:LANES + 1]


def _mla_attn_body(q_ref, k_ref, vt_ref, o_ref, *scr):
    qi = pl.program_id(2)
    qt_ref = scr[0]
    heads = ATT_LANES // ATT_TQ
    for h in range(heads):
        qt_ref[:, h * ATT_TQ:(h + 1) * ATT_TQ] = q_ref[0, :, h * LANES:(h + 1) * LANES].astype(F32).T.astype(BF16)
    per_head = ATT_TQ // ATT_STRIP
    ot = _flash_t(k_ref, vt_ref, qt_ref, qi, scr[1:], lambda c: c // per_head, lambda c: c // (2 * per_head))
    row = lax.broadcasted_iota(jnp.int32, (LANES, ATT_TQ), 0)
    for r in range(heads // 2):
        lo = ot[:, 2 * r * ATT_TQ:(2 * r + 1) * ATT_TQ]
        hi = ot[:, (2 * r + 1) * ATT_TQ:(2 * r + 2) * ATT_TQ]
        o_ref[0, :, r * LANES:(r + 1) * LANES] = jnp.where(row < MLA_V, lo, hi).T.astype(BF16)


def _attn_specs(s, width):
    npair = s // ATT_TQ
    n = ATT_LANES
    row = pltpu.VMEM((1, n), F32)
    return dict(
        in_specs=[pl.BlockSpec((1, ATT_TQ, width), lambda b, h, i: (b, i, h)),
                  pl.BlockSpec((1, s, width), lambda b, h, i: (b, 0, h)),
                  pl.BlockSpec((1, npair, 2 * LANES, ATT_TQ), lambda b, h, i: (b, 0, h, 0))],
        out_specs=pl.BlockSpec((1, ATT_TQ, 2 * LANES), lambda b, h, i: (b, i, h)),
        scratch_shapes=[pltpu.VMEM((LANES, n), BF16),
                        pltpu.VMEM((ATT_TK, n), F32), pltpu.VMEM((ATT_TK, n), F32),
                        pltpu.VMEM((ATT_TK, n), BF16), pltpu.VMEM((ATT_TK, n), BF16),
                        row, row, row, pltpu.VMEM((LANES + ATT_SUM_ROWS, n), F32)],
        compiler_params=_params(("parallel", "parallel", "arbitrary")))


def _mla_attn(q, k, vt):
    nb, s, _ = q.shape
    heads = ATT_LANES // ATT_TQ
    return pl.pallas_call(
        _mla_attn_body,
        grid=(nb, MLA_HEADS // heads, s // ATT_TQ),
        out_shape=jax.ShapeDtypeStruct((nb, s, MLA_HEADS * MLA_V), BF16),
        name="mla_attn", **_attn_specs(s, heads * LANES),
    )(q, k, vt)


def _diff_attn_body(lam_ref, q_ref, k_ref, vt_ref, sub_ref, o_ref, *scr, out_scale):
    qi = pl.program_id(2)
    qt_ref = scr[0]
    heads = ATT_LANES // (2 * ATT_TQ)
    for g in range(heads):
        qt = q_ref[0, :, g * LANES:(g + 1) * LANES].astype(F32).T
        row = lax.broadcasted_iota(jnp.int32, qt.shape, 0)
        zero = jnp.zeros_like(qt)
        qt_ref[:, 2 * g * ATT_TQ:(2 * g + 1) * ATT_TQ] = jnp.where(row < DIFF_HEAD_DIM, qt, zero).astype(BF16)
        qt_ref[:, (2 * g + 1) * ATT_TQ:(2 * g + 2) * ATT_TQ] = jnp.where(row < DIFF_HEAD_DIM, zero, qt).astype(BF16)
    per_head = 2 * ATT_TQ // ATT_STRIP
    ot = _flash_t(k_ref, vt_ref, qt_ref, qi, scr[1:], lambda c: c // per_head, lambda c: c // per_head)
    for g in range(heads):
        o = (ot[:, 2 * g * ATT_TQ:(2 * g + 1) * ATT_TQ]
             - lam_ref[0, 0] * ot[:, (2 * g + 1) * ATT_TQ:(2 * g + 2) * ATT_TQ]).T
        o_ref[0, :, g * LANES:(g + 1) * LANES] = (_rms(o, sub_ref[...]) * out_scale).astype(BF16)


def _diff_attn(lam, q, k, vt, subln, out_scale):
    nb, s, _ = q.shape
    heads = ATT_LANES // (2 * ATT_TQ)
    spec = _attn_specs(s, heads * LANES)
    spec["in_specs"] = ([pl.BlockSpec(memory_space=pltpu.SMEM)] + spec["in_specs"]
                        + [pl.BlockSpec((1, LANES), lambda b, h, i: (0, 0))])
    return pl.pallas_call(
        functools.partial(_diff_attn_body, out_scale=out_scale),
        grid=(nb, DIFF_HEADS // heads, s // ATT_TQ),
        out_shape=jax.ShapeDtypeStruct((nb, s, DIFF_HEADS * 2 * DIFF_HEAD_DIM), BF16),
        name="diff_attn", **spec,
    )(lam, q, k, vt, subln.reshape(1, -1))


def _rope_table(positions, rot_dim, theta, first_lo, period):
    half = rot_dim // 2
    inv = theta ** (-jnp.arange(half, dtype=F32) * 2.0 / rot_dim)
    ang = positions.astype(F32).reshape(-1, 1) * inv
    cos, sin = jnp.cos(ang), jnp.sin(ang)
    t = ang.shape[0]
    pad_lo = jnp.zeros((t, first_lo), F32)
    pad_hi = jnp.zeros((t, period - first_lo - rot_dim), F32)
    cos_p = jnp.concatenate([pad_lo + 1.0, cos, cos, pad_hi + 1.0], axis=1)
    sin_p = jnp.concatenate([pad_lo, -sin, sin, pad_hi], axis=1)
    reps = LANES // period
    return jnp.tile(cos_p, (1, reps)), jnp.tile(sin_p, (1, reps))


def _s5_discretize(lam_re, lam_im, log_dt, b_re, b_im):
    dt = jnp.exp(log_dt)[:, None]
    mag = jnp.exp(lam_re * dt)
    ang = lam_im * dt
    ab_re = mag * jnp.cos(ang)
    ab_im = mag * jnp.sin(ang)
    den = lam_re * lam_re + lam_im * lam_im
    nr = ab_re - 1.0
    f_re = (nr * lam_re + ab_im * lam_im) / den
    f_im = (ab_im * lam_re - nr * lam_im) / den
    bb_re = f_re[..., None] * b_re - f_im[..., None] * b_im
    bb_im = f_re[..., None] * b_im + f_im[..., None] * b_re
    return ab_re, ab_im, bb_re, bb_im


def _block_diag_in(bb):
    eye = jnp.eye(S5_GROUPS, dtype=F32)
    return jnp.einsum('gph,gk->ghkp', bb, eye).reshape(S5_WIDTH, S5_FEATS)


def _block_diag_out(c):
    eye = jnp.eye(S5_GROUPS, dtype=F32)
    return jnp.einsum('ghp,gk->gpkh', c, eye).reshape(S5_FEATS, S5_WIDTH)


def _pad_heads(w, heads, width, lo, hi):
    kdim = w.shape[0]
    w = w.reshape(kdim, heads, width)[:, :, lo:hi]
    w = jnp.pad(w, ((0, 0), (0, 0), (0, LANES - (hi - lo))))
    return w.reshape(kdim, heads * LANES)


def kernel(x, positions, ffn_norm, ffn_w_gu, ffn_w_down, ev_norm, ev_w_in, s5_lambda_re, s5_lambda_im, s5_log_dt, s5_b_re, s5_b_im, s5_c_re, s5_c_im, s5_d, s5_w_glu, s5_b_glu, mla_q_norm, mla_w_uq, mla_kv_norm, mla_w_ukv, ev_w_out, od_norm, od_w_in, diff_lq1, diff_lk1, diff_lq2, diff_lk2, diff_subln, od_w_out, final_norm):
    nb, s, d = x.shape
    t = nb * s
    xt = x.reshape(t, d)
    wgu = ffn_w_gu.astype(BF16)
    wdn = ffn_w_down.astype(BF16)

    xt = _ffn(xt, [], ffn_norm[0, 0], wgu[0, 0], wdn[0, 0])

    cos_m, sin_m = _rope_table(positions, MLA_ROPE, MLA_ROPE_THETA, MLA_NOPE, LANES)
    w_in = ev_w_in[0]
    o_kr = S5_WIDTH + MLA_Q_LORA + MLA_KV_LORA
    w_kr = jnp.pad(w_in[:, o_kr:], ((0, 0), (MLA_NOPE, LANES - MLA_NOPE - MLA_ROPE)))
    win_p = jnp.concatenate([w_in[:, :o_kr], w_kr], axis=1).astype(BF16)
    wuq_p = _pad_heads(mla_w_uq[0], MLA_HEADS, MLA_NOPE + MLA_ROPE, 0, MLA_NOPE + MLA_ROPE).astype(BF16)
    wuk_p = _pad_heads(mla_w_ukv[0], MLA_HEADS, MLA_NOPE + MLA_V, 0, MLA_NOPE).astype(BF16)
    wuv = mla_w_ukv[0].reshape(MLA_KV_LORA, MLA_HEADS, MLA_NOPE + MLA_V)[:, :, MLA_NOPE:]
    wuvt = wuv.reshape(MLA_KV_LORA, MLA_HEADS * MLA_V).T.astype(BF16)
    u, q, k, vt = _even_prep(xt, ev_norm[0], win_p, mla_q_norm[0], wuq_p, mla_kv_norm[0], wuk_p, wuvt,
                             cos_m, sin_m, nb)

    ab_re, ab_im, bb_re, bb_im = _s5_discretize(s5_lambda_re[0], s5_lambda_im[0], s5_log_dt[0], s5_b_re[0], s5_b_im[0])
    cmat = jnp.concatenate([_block_diag_out(s5_c_re[0]), -_block_diag_out(s5_c_im[0])], axis=0).astype(BF16)
    s5_out = _s5(u.reshape(nb, s, S5_WIDTH), ab_re.reshape(S5_ROWS, LANES), ab_im.reshape(S5_ROWS, LANES),
                 _block_diag_in(bb_re).astype(BF16), _block_diag_in(bb_im).astype(BF16), cmat,
                 s5_d[0].reshape(-1), s5_w_glu[0].astype(BF16), s5_b_glu[0])

    o = _mla_attn(q.reshape(nb, s, -1), k.reshape(nb, s, -1), vt)
    w_out = ev_w_out[0].astype(BF16)
    xt = _ffn(xt, [(s5_out.reshape(t, S5_WIDTH), w_out[:S5_WIDTH]), (o.reshape(t, -1), w_out[S5_WIDTH:])],
              ffn_norm[0, 1], wgu[0, 1], wdn[0, 1])

    xt = _ffn(xt, [], ffn_norm[1, 0], wgu[1, 0], wdn[1, 0])
    cos_t, sin_t = _rope_table(positions, DIFF_ROT, ROPE_THETA, 0, DIFF_HEAD_DIM)
    width = DIFF_HEADS * 2 * DIFF_HEAD_DIM
    w_in = od_w_in[0]
    q, k, vt = _odd_prep(xt, od_norm[0], w_in[:, :2 * width].astype(BF16), w_in[:, 2 * width:].T.astype(BF16),
                         cos_t, sin_t, nb)
    lam_init = 0.8 - 0.6 * math.exp(-0.3 * 1)
    lam = (jnp.exp(jnp.sum(diff_lq1[0] * diff_lk1[0])) - jnp.exp(jnp.sum(diff_lq2[0] * diff_lk2[0])) + lam_init)
    o = _diff_attn(lam.reshape(1, 1), q.reshape(nb, s, -1), k.reshape(nb, s, -1), vt, diff_subln[0],
                   1.0 - lam_init)
    xt = _ffn(xt, [(o.reshape(t, -1), od_w_out[0].astype(BF16))], ffn_norm[1, 1], wgu[1, 1], wdn[1, 1],
              gfinal=final_norm)
    return xt.reshape(nb, s, d)
```

```python
import functools
import math

import jax
import jax.numpy as jnp
from jax import lax
from jax.experimental import pallas as pl
from jax.experimental.pallas import tpu as pltpu

F32 = jnp.float32
BF16 = jnp.bfloat16

D_MODEL = 1024
NORM_EPS = 1e-6
D_FF = 2816
S5_WIDTH = 512
S5_GROUP_CH = 16
S5_GROUPS = 32
S5_STATE = 64
S5_FEATS = S5_GROUPS * S5_STATE
MLA_HEADS = 8
MLA_NOPE = 64
MLA_ROPE = 32
MLA_V = 64
MLA_Q_LORA = 384
MLA_KV_LORA = 256
MLA_ROPE_THETA = 10000.0
DIFF_HEADS = 8
DIFF_HEAD_DIM = 64
DIFF_ROT = 16
ROPE_THETA = 500000.0

LANES = 128
SUBLANES = 8
VMEM_LIMIT = 56 * 1024 * 1024
LOG2E = math.log2(math.e)

FFN_TM = 512
FFN_CHUNKS = 2
S5_TS = 256
S5_PITCH = S5_TS + SUBLANES
S5_ROWS = S5_FEATS // LANES
ATT_TQ = 512
ATT_TK = ATT_TQ // 2
ATT_LANES = 4 * ATT_TQ
ATT_STRIP = 256
ATT_SUM_ROWS = 16
ATT_PAD = LANES
PREP_TM = ATT_TQ

_NT = (((1,), (1,)), ((), ()))


def _rms(x, g):
    ms = jnp.mean(x * x, axis=-1, keepdims=True)
    return x * lax.rsqrt(ms + NORM_EPS) * g


def _const_spec(shape):
    nd = len(shape)
    return pl.BlockSpec(shape, lambda *_: (0,) * nd, pipeline_mode=pl.Buffered(1))


def _params(sem, flags=None):
    return pltpu.CompilerParams(dimension_semantics=sem, vmem_limit_bytes=VMEM_LIMIT, flags=flags)


def _ffn_body(*refs, n_mix, final):
    x_ref = refs[0]
    mix_refs = refs[1:1 + 2 * n_mix]
    g_ref, wgu_ref, wd_ref = refs[1 + 2 * n_mix:4 + 2 * n_mix]
    gf_ref = refs[4 + 2 * n_mix] if final else None
    o_ref = refs[-1]

    x = x_ref[...]
    for i in range(n_mix):
        x = x + jnp.dot(mix_refs[2 * i][...], mix_refs[2 * i + 1][...], preferred_element_type=F32)
    h = _rms(x, g_ref[...]).astype(BF16)
    ck = D_FF // FFN_CHUNKS
    y = None
    for c in range(FFN_CHUNKS):
        g = jnp.dot(h, wgu_ref[:, c * ck:(c + 1) * ck], preferred_element_type=F32)
        u = jnp.dot(h, wgu_ref[:, D_FF + c * ck:D_FF + (c + 1) * ck], preferred_element_type=F32)
        a = (g * jax.nn.sigmoid(g) * u).astype(BF16)
        d = jnp.dot(a, wd_ref[c * ck:(c + 1) * ck, :], preferred_element_type=F32)
        y = d if y is None else y + d
    out = x + 0.5 * y
    if final:
        out = _rms(out, gf_ref[...])
    o_ref[...] = out


def _ffn(x, mixes, g, wgu, wd, gfinal=None):
    t = x.shape[0]
    n_mix = len(mixes)
    final = gfinal is not None
    in_specs = [pl.BlockSpec((FFN_TM, D_MODEL), lambda i: (i, 0))]
    args = [x]
    for m, w in mixes:
        in_specs += [pl.BlockSpec((FFN_TM, m.shape[1]), lambda i: (i, 0)), _const_spec(w.shape)]
        args += [m, w]
    in_specs += [_const_spec((1, D_MODEL)), _const_spec(wgu.shape), _const_spec(wd.shape)]
    args += [g.reshape(1, D_MODEL), wgu, wd]
    if final:
        in_specs.append(_const_spec((1, D_MODEL)))
        args.append(gfinal.reshape(1, D_MODEL))
    return pl.pallas_call(
        functools.partial(_ffn_body, n_mix=n_mix, final=final),
        grid=(t // FFN_TM,),
        in_specs=in_specs,
        out_specs=pl.BlockSpec((FFN_TM, D_MODEL), lambda i: (i, 0)),
        out_shape=jax.ShapeDtypeStruct((t, D_MODEL), F32),
        compiler_params=_params(("parallel",)),
        name="ffn",
    )(*args)


def _rope(blk, cos, sin, half, first_lo, period):
    lane = lax.broadcasted_iota(jnp.int32, blk.shape, 1) % period
    fwd = pltpu.roll(blk, LANES - half, 1)
    bwd = pltpu.roll(blk, half, 1)
    rot = jnp.where(lane < first_lo + half, fwd, bwd)
    return blk * cos + rot * sin


def _even_prep_body(x_ref, g_ref, win_ref, qn_ref, wuq_ref, kvn_ref, wuk_ref, wuvt_ref, cos_ref, sin_ref,
                    u_ref, q_ref, k_ref, vt_ref):
    h = _rms(x_ref[...], g_ref[...]).astype(BF16)
    proj = jnp.dot(h, win_ref[...], preferred_element_type=F32)
    o_cq = S5_WIDTH
    o_ckv = o_cq + MLA_Q_LORA
    o_kr = o_ckv + MLA_KV_LORA
    u_ref[...] = proj[:, :o_cq]
    cq = proj[:, o_cq:o_ckv]
    ckv = proj[:, o_ckv:o_kr]
    kr = proj[:, o_kr:o_kr + LANES]
    cos = cos_ref[...]
    sin = sin_ref[...]
    half = MLA_ROPE // 2
    scale = (MLA_NOPE + MLA_ROPE) ** -0.5 * LOG2E
    q = jnp.dot(_rms(cq, qn_ref[...]).astype(BF16), wuq_ref[...], preferred_element_type=F32)
    ckn = _rms(ckv, kvn_ref[...]).astype(BF16)
    kn = jnp.dot(ckn, wuk_ref[...], preferred_element_type=F32)
    vt_ref[0, 0] = lax.dot_general(wuvt_ref[...], ckn, _NT, preferred_element_type=F32).astype(BF16)
    kpe = _rope(kr, cos, sin, half, MLA_NOPE, LANES)
    for hd in range(MLA_HEADS):
        sl = slice(hd * LANES, (hd + 1) * LANES)
        q_ref[:, sl] = (_rope(q[:, sl], cos, sin, half, MLA_NOPE, LANES) * scale).astype(BF16)
        k_ref[:, sl] = (kn[:, sl] + kpe).astype(BF16)


def _vt_spec(nblk, rows):
    return pl.BlockSpec((1, 1, rows, PREP_TM), lambda i: (i // nblk, i % nblk, 0, 0))


def _even_prep(x, g, win, qn, wuq, kvn, wuk, wuvt, cos, sin, nb):
    t = x.shape[0]
    tm = PREP_TM
    nblk = t // nb // tm
    row = lambda w: pl.BlockSpec((tm, w), lambda i: (i, 0))
    return pl.pallas_call(
        _even_prep_body,
        grid=(t // tm,),
        in_specs=[row(D_MODEL), _const_spec((1, D_MODEL)), _const_spec(win.shape),
                  _const_spec((1, MLA_Q_LORA)), _const_spec(wuq.shape),
                  _const_spec((1, MLA_KV_LORA)), _const_spec(wuk.shape), _const_spec(wuvt.shape),
                  row(LANES), row(LANES)],
        out_specs=[row(S5_WIDTH), row(MLA_HEADS * LANES), row(MLA_HEADS * LANES),
                   _vt_spec(nblk, MLA_HEADS * MLA_V)],
        out_shape=[jax.ShapeDtypeStruct((t, S5_WIDTH), F32),
                   jax.ShapeDtypeStruct((t, MLA_HEADS * LANES), BF16),
                   jax.ShapeDtypeStruct((t, MLA_HEADS * LANES), BF16),
                   jax.ShapeDtypeStruct((nb, nblk, MLA_HEADS * MLA_V, tm), BF16)],
        compiler_params=_params(("parallel",)),
        name="even_prep",
    )(x, g.reshape(1, -1), win, qn.reshape(1, -1), wuq, kvn.reshape(1, -1), wuk, wuvt, cos, sin)


def _odd_prep_body(x_ref, g_ref, wqk_ref, wvt_ref, cos_ref, sin_ref, q_ref, k_ref, vt_ref):
    h = _rms(x_ref[...], g_ref[...]).astype(BF16)
    cos = cos_ref[...]
    sin = sin_ref[...]
    half = DIFF_ROT // 2
    width = DIFF_HEADS * 2 * DIFF_HEAD_DIM
    scale = DIFF_HEAD_DIM ** -0.5 * LOG2E
    q = jnp.dot(h, wqk_ref[:, :width], preferred_element_type=F32)
    k = jnp.dot(h, wqk_ref[:, width:], preferred_element_type=F32)
    vt_ref[0, 0] = lax.dot_general(wvt_ref[...], h, _NT, preferred_element_type=F32).astype(BF16)
    for c in range(width // LANES):
        sl = slice(c * LANES, (c + 1) * LANES)
        q_ref[:, sl] = (_rope(q[:, sl], cos, sin, half, 0, DIFF_HEAD_DIM) * scale).astype(BF16)
        k_ref[:, sl] = _rope(k[:, sl], cos, sin, half, 0, DIFF_HEAD_DIM).astype(BF16)


def _odd_prep(x, g, wqk, wvt, cos, sin, nb):
    t = x.shape[0]
    tm = PREP_TM
    nblk = t // nb // tm
    width = DIFF_HEADS * 2 * DIFF_HEAD_DIM
    row = lambda w: pl.BlockSpec((tm, w), lambda i: (i, 0))
    return pl.pallas_call(
        _odd_prep_body,
        grid=(t // tm,),
        in_specs=[row(D_MODEL), _const_spec((1, D_MODEL)), _const_spec(wqk.shape), _const_spec(wvt.shape),
                  row(LANES), row(LANES)],
        out_specs=[row(width), row(width), _vt_spec(nblk, width)],
        out_shape=[jax.ShapeDtypeStruct((t, width), BF16), jax.ShapeDtypeStruct((t, width), BF16),
                   jax.ShapeDtypeStruct((nb, nblk, width, tm), BF16)],
        compiler_params=_params(("parallel",)),
        name="odd_prep",
    )(x, g.reshape(1, -1), wqk, wvt, cos, sin)


def _s5_body(u_ref, ar_ref, ai_ref, bre_ref, bim_ref, c_ref, d_ref, wglu_ref, bglu_ref, o_ref,
             xr_ref, xi_ref, sr_ref, si_ref):
    nb = u_ref.shape[0]

    @pl.when(pl.program_id(0) == 0)
    def _():
        sr_ref[...] = jnp.zeros_like(sr_ref)
        si_ref[...] = jnp.zeros_like(si_ref)

    for b in range(nb):
        ub = u_ref[b].astype(BF16)
        bur = jnp.dot(ub, bre_ref[...], preferred_element_type=F32)
        bui = jnp.dot(ub, bim_ref[...], preferred_element_type=F32)
        for j in range(S5_ROWS):
            xr_ref[b, j * S5_PITCH:j * S5_PITCH + S5_TS, :] = bur[:, j * LANES:(j + 1) * LANES]
            xi_ref[b, j * S5_PITCH:j * S5_PITCH + S5_TS, :] = bui[:, j * LANES:(j + 1) * LANES]

    ar = ar_ref[...]
    ai = ai_ref[...]

    def step(t, carry):
        new = []
        for b in range(nb):
            pr, pi = carry[2 * b], carry[2 * b + 1]
            idx = pl.ds(t, S5_ROWS, stride=S5_PITCH)
            nr = ar * pr - ai * pi + xr_ref[b, idx, :]
            ni = ar * pi + ai * pr + xi_ref[b, idx, :]
            xr_ref[b, idx, :] = nr
            xi_ref[b, idx, :] = ni
            new += [nr, ni]
        return tuple(new)

    init = []
    for b in range(nb):
        init += [sr_ref[b], si_ref[b]]
    fin = lax.fori_loop(0, S5_TS, step, tuple(init), unroll=4)
    for b in range(nb):
        sr_ref[b] = fin[2 * b]
        si_ref[b] = fin[2 * b + 1]

    for b in range(nb):
        parts = [xr_ref[b, j * S5_PITCH:j * S5_PITCH + S5_TS, :].astype(BF16) for j in range(S5_ROWS)]
        parts += [xi_ref[b, j * S5_PITCH:j * S5_PITCH + S5_TS, :].astype(BF16) for j in range(S5_ROWS)]
        xs = jnp.concatenate(parts, axis=1)
        y = jnp.dot(xs, c_ref[...], preferred_element_type=F32) + d_ref[...] * u_ref[b]
        g = 0.5 * y * (1.0 + jnp.tanh(math.sqrt(2.0 / math.pi) * (y + 0.044715 * (y * y * y))))
        z = jnp.dot(g.astype(BF16), wglu_ref[...], preferred_element_type=F32) + bglu_ref[...]
        o_ref[b] = (g * jax.nn.sigmoid(z)).astype(BF16)


def _s5(u, ar, ai, bre, bim, cmat, dskip, wglu, bglu):
    nb, s, _ = u.shape
    blk = pl.BlockSpec((nb, S5_TS, S5_WIDTH), lambda i: (0, i, 0))
    return pl.pallas_call(
        _s5_body,
        grid=(s // S5_TS,),
        in_specs=[blk, _const_spec(ar.shape), _const_spec(ai.shape), _const_spec(bre.shape),
                  _const_spec(bim.shape), _const_spec(cmat.shape), _const_spec((1, S5_WIDTH)),
                  _const_spec(wglu.shape), _const_spec((1, S5_WIDTH))],
        out_specs=blk,
        out_shape=jax.ShapeDtypeStruct((nb, s, S5_WIDTH), BF16),
        scratch_shapes=[pltpu.VMEM((nb, S5_ROWS * S5_PITCH, LANES), F32),
                        pltpu.VMEM((nb, S5_ROWS * S5_PITCH, LANES), F32),
                        pltpu.VMEM((nb, S5_ROWS, LANES), F32),
                        pltpu.VMEM((nb, S5_ROWS, LANES), F32)],
        compiler_params=_params(("arbitrary",)),
        name="s5",
    )(u, ar, ai, bre, bim, cmat, dskip.reshape(1, -1), wglu, bglu.reshape(1, -1))


def _flash_t(k_ref, vt_ref, qt_ref, qi, scr, k_block, v_block):
    s0, s1, p0, p1, a0, a1, m_ref, acc_ref = scr
    strips = tuple(range(ATT_LANES // ATT_STRIP))

    def q_off(c):
        return (c * ATT_STRIP) % ATT_TQ

    lower = tuple(c for c in strips if q_off(c) < ATT_TK)
    upper = tuple(c for c in strips if q_off(c) >= ATT_TK)
    m_ref[...] = jnp.full(m_ref.shape, -jnp.inf, F32)
    acc_ref[...] = jnp.zeros(acc_ref.shape, F32)
    p1[...] = jnp.zeros(p1.shape, BF16)
    a1[...] = jnp.ones(a1.shape, F32)
    ones = jnp.ones((ATT_SUM_ROWS, ATT_TK), BF16)

    def scores(j, s_ref, which):
        for c in which:
            sl = slice(c * ATT_STRIP, (c + 1) * ATT_STRIP)
            kb = k_ref[0, pl.ds(j * ATT_TK, ATT_TK), k_block(c) * LANES:(k_block(c) + 1) * LANES]
            s_ref[:, sl] = jnp.dot(kb, qt_ref[:, sl], preferred_element_type=F32)

    def softmax(s_ref, p_ref, a_ref, which, key_off):
        for c in which:
            for t in range(ATT_STRIP // LANES):
                sl = slice(c * ATT_STRIP + t * LANES, c * ATT_STRIP + (t + 1) * LANES)
                s = s_ref[:, sl]
                if key_off is not None:
                    kpos = lax.broadcasted_iota(jnp.int32, s.shape, 0) + key_off
                    qpos = lax.broadcasted_iota(jnp.int32, s.shape, 1) + (q_off(c) + t * LANES)
                    s = jnp.where(kpos <= qpos, s, -jnp.inf)
                m_old = m_ref[:, sl]
                m_new = jnp.maximum(m_old, jnp.max(s, axis=0, keepdims=True))
                a_ref[:, sl] = jnp.exp2(m_old - m_new)
                m_ref[:, sl] = m_new
                p_ref[:, sl] = jnp.exp2(s - m_new).astype(BF16)

    def value(pair, half, p_ref, a_ref, which):
        for c in which:
            sl = slice(c * ATT_STRIP, (c + 1) * ATT_STRIP)
            vt = vt_ref[0, pair, v_block(c) * LANES:(v_block(c) + 1) * LANES, half * ATT_TK:(half + 1) * ATT_TK]
            pv = jnp.dot(jnp.concatenate([vt, ones], axis=0), p_ref[:, sl], preferred_element_type=F32)
            acc_ref[:, sl] = a_ref[:, sl] * acc_ref[:, sl] + pv

    scores(0, s0, strips)

    def pair_step(i, carry):
        softmax(s0, p0, a0, strips, None)
        scores(2 * i + 1, s1, strips)
        value(jnp.maximum(i - 1, 0), 1, p1, a1, strips)
        softmax(s1, p1, a1, strips, None)
        scores(2 * i + 2, s0, strips)
        value(i, 0, p0, a0, strips)
        return carry

    lax.fori_loop(0, qi, pair_step, 0)
    scores(2 * qi + 1, s1, upper)
    softmax(s0, p0, a0, lower, 0)
    softmax(s0, p0, a0, upper, None)
    value(jnp.maximum(qi - 1, 0), 1, p1, a1, strips)
    softmax(s1, p1, a1, upper, ATT_TK)
    value(qi, 0, p0, a0, strips)
    value(qi, 1, p1, a1, upper)
    return acc_ref[:LANES, :ATT_LANES] / acc_ref[LANES:LANES + 1, :ATT_LANES]


def _mla_attn_body(q_ref, k_ref, vt_ref, o_ref, *scr):
    qi = pl.program_id(2)
    qt_ref = scr[0]
    heads = ATT_LANES // ATT_TQ
    for h in range(heads):
        qt_ref[:, h * ATT_TQ:(h + 1) * ATT_TQ] = q_ref[0, :, h * LANES:(h + 1) * LANES].astype(F32).T.astype(BF16)
    per_head = ATT_TQ // ATT_STRIP
    ot = _flash_t(k_ref, vt_ref, qt_ref, qi, scr[1:], lambda c: c // per_head, lambda c: c // (2 * per_head))
    row = lax.broadcasted_iota(jnp.int32, (LANES, ATT_TQ), 0)
    for r in range(heads // 2):
        lo = ot[:, 2 * r * ATT_TQ:(2 * r + 1) * ATT_TQ]
        hi = ot[:, (2 * r + 1) * ATT_TQ:(2 * r + 2) * ATT_TQ]
        o_ref[0, :, r * LANES:(r + 1) * LANES] = jnp.where(row < MLA_V, lo, hi).T.astype(BF16)


def _attn_specs(s, qk_width, v_width):
    n = ATT_LANES
    row = pltpu.VMEM((1, n), F32)
    return dict(
        in_specs=[pl.BlockSpec((1, ATT_TQ, qk_width), lambda b, h, i: (b, i, h)),
                  pl.BlockSpec((1, s, qk_width), lambda b, h, i: (b, 0, h)),
                  pl.BlockSpec((1, s // ATT_TQ, v_width, ATT_TQ), lambda b, h, i: (b, 0, h, 0))],
        out_specs=pl.BlockSpec((1, ATT_TQ, v_width), lambda b, h, i: (b, i, h)),
        scratch_shapes=[pltpu.VMEM((LANES, n), BF16),
                        pltpu.VMEM((ATT_TK, n + ATT_PAD), F32), pltpu.VMEM((ATT_TK, n + ATT_PAD), F32),
                        pltpu.VMEM((ATT_TK, n), BF16), pltpu.VMEM((ATT_TK, n), BF16),
                        row, row, row, pltpu.VMEM((LANES + ATT_SUM_ROWS, n + ATT_PAD), F32)],
        compiler_params=_params(("parallel", "parallel", "arbitrary")))


def _mla_attn(q, k, vt):
    nb, s, _ = q.shape
    heads = ATT_LANES // ATT_TQ
    return pl.pallas_call(
        _mla_attn_body,
        grid=(nb, MLA_HEADS // heads, s // ATT_TQ),
        out_shape=jax.ShapeDtypeStruct((nb, s, MLA_HEADS * MLA_V), BF16),
        name="mla_attn", **_attn_specs(s, heads * LANES, heads * MLA_V),
    )(q, k, vt)


def _diff_attn_body(lam_ref, q_ref, k_ref, vt_ref, sub_ref, o_ref, *scr, out_scale):
    qi = pl.program_id(2)
    qt_ref = scr[0]
    heads = ATT_LANES // (2 * ATT_TQ)
    for g in range(heads):
        qt = q_ref[0, :, g * LANES:(g + 1) * LANES].astype(F32).T
        row = lax.broadcasted_iota(jnp.int32, qt.shape, 0)
        zero = jnp.zeros_like(qt)
        qt_ref[:, 2 * g * ATT_TQ:(2 * g + 1) * ATT_TQ] = jnp.where(row < DIFF_HEAD_DIM, qt, zero).astype(BF16)
        qt_ref[:, (2 * g + 1) * ATT_TQ:(2 * g + 2) * ATT_TQ] = jnp.where(row < DIFF_HEAD_DIM, zero, qt).astype(BF16)
    per_head = 2 * ATT_TQ // ATT_STRIP
    ot = _flash_t(k_ref, vt_ref, qt_ref, qi, scr[1:], lambda c: c // per_head, lambda c: c // per_head)
    for g in range(heads):
        o = (ot[:, 2 * g * ATT_TQ:(2 * g + 1) * ATT_TQ]
             - lam_ref[0, 0] * ot[:, (2 * g + 1) * ATT_TQ:(2 * g + 2) * ATT_TQ]).T
        o_ref[0, :, g * LANES:(g + 1) * LANES] = (_rms(o, sub_ref[...]) * out_scale).astype(BF16)


def _diff_attn(lam, q, k, vt, subln, out_scale):
    nb, s, _ = q.shape
    heads = ATT_LANES // (2 * ATT_TQ)
    spec = _attn_specs(s, heads * LANES, heads * LANES)
    spec["in_specs"] = ([pl.BlockSpec(memory_space=pltpu.SMEM)] + spec["in_specs"]
                        + [pl.BlockSpec((1, LANES), lambda b, h, i: (0, 0))])
    return pl.pallas_call(
        functools.partial(_diff_attn_body, out_scale=out_scale),
        grid=(nb, DIFF_HEADS // heads, s // ATT_TQ),
        out_shape=jax.ShapeDtypeStruct((nb, s, DIFF_HEADS * 2 * DIFF_HEAD_DIM), BF16),
        name="diff_attn", **spec,
    )(lam, q, k, vt, subln.reshape(1, -1))


def _rope_table(positions, rot_dim, theta, first_lo, period):
    half = rot_dim // 2
    inv = theta ** (-jnp.arange(half, dtype=F32) * 2.0 / rot_dim)
    ang = positions.astype(F32).reshape(-1, 1) * inv
    cos, sin = jnp.cos(ang), jnp.sin(ang)
    t = ang.shape[0]
    pad_lo = jnp.zeros((t, first_lo), F32)
    pad_hi = jnp.zeros((t, period - first_lo - rot_dim), F32)
    cos_p = jnp.concatenate([pad_lo + 1.0, cos, cos, pad_hi + 1.0], axis=1)
    sin_p = jnp.concatenate([pad_lo, -sin, sin, pad_hi], axis=1)
    reps = LANES // period
    return jnp.tile(cos_p, (1, reps)), jnp.tile(sin_p, (1, reps))


def _s5_discretize(lam_re, lam_im, log_dt, b_re, b_im):
    dt = jnp.exp(log_dt)[:, None]
    mag = jnp.exp(lam_re * dt)
    ang = lam_im * dt
    ab_re = mag * jnp.cos(ang)
    ab_im = mag * jnp.sin(ang)
    den = lam_re * lam_re + lam_im * lam_im
    nr = ab_re - 1.0
    f_re = (nr * lam_re + ab_im * lam_im) / den
    f_im = (ab_im * lam_re - nr * lam_im) / den
    bb_re = f_re[..., None] * b_re - f_im[..., None] * b_im
    bb_im = f_re[..., None] * b_im + f_im[..., None] * b_re
    return ab_re, ab_im, bb_re, bb_im


def _block_diag_in(bb):
    eye = jnp.eye(S5_GROUPS, dtype=F32)
    return jnp.einsum('gph,gk->ghkp', bb, eye).reshape(S5_WIDTH, S5_FEATS)


def _block_diag_out(c):
    eye = jnp.eye(S5_GROUPS, dtype=F32)
    return jnp.einsum('ghp,gk->gpkh', c, eye).reshape(S5_FEATS, S5_WIDTH)


def _pad_heads(w, heads, width, lo, hi):
    kdim = w.shape[0]
    w = w.reshape(kdim, heads, width)[:, :, lo:hi]
    w = jnp.pad(w, ((0, 0), (0, 0), (0, LANES - (hi - lo))))
    return w.reshape(kdim, heads * LANES)


def kernel(x, positions, ffn_norm, ffn_w_gu, ffn_w_down, ev_norm, ev_w_in, s5_lambda_re, s5_lambda_im, s5_log_dt, s5_b_re, s5_b_im, s5_c_re, s5_c_im, s5_d, s5_w_glu, s5_b_glu, mla_q_norm, mla_w_uq, mla_kv_norm, mla_w_ukv, ev_w_out, od_norm, od_w_in, diff_lq1, diff_lk1, diff_lq2, diff_lk2, diff_subln, od_w_out, final_norm):
    nb, s, d = x.shape
    t = nb * s
    xt = x.reshape(t, d)
    wgu = ffn_w_gu.astype(BF16)
    wdn = ffn_w_down.astype(BF16)

    xt = _ffn(xt, [], ffn_norm[0, 0], wgu[0, 0], wdn[0, 0])

    cos_m, sin_m = _rope_table(positions, MLA_ROPE, MLA_ROPE_THETA, MLA_NOPE, LANES)
    w_in = ev_w_in[0]
    o_kr = S5_WIDTH + MLA_Q_LORA + MLA_KV_LORA
    w_kr = jnp.pad(w_in[:, o_kr:], ((0, 0), (MLA_NOPE, LANES - MLA_NOPE - MLA_ROPE)))
    win_p = jnp.concatenate([w_in[:, :o_kr], w_kr], axis=1).astype(BF16)
    wuq_p = _pad_heads(mla_w_uq[0], MLA_HEADS, MLA_NOPE + MLA_ROPE, 0, MLA_NOPE + MLA_ROPE).astype(BF16)
    wuk_p = _pad_heads(mla_w_ukv[0], MLA_HEADS, MLA_NOPE + MLA_V, 0, MLA_NOPE).astype(BF16)
    wuv = mla_w_ukv[0].reshape(MLA_KV_LORA, MLA_HEADS, MLA_NOPE + MLA_V)[:, :, MLA_NOPE:]
    wuvt = wuv.reshape(MLA_KV_LORA, MLA_HEADS * MLA_V).T.astype(BF16)
    u, q, k, vt = _even_prep(xt, ev_norm[0], win_p, mla_q_norm[0], wuq_p, mla_kv_norm[0], wuk_p, wuvt,
                             cos_m, sin_m, nb)

    ab_re, ab_im, bb_re, bb_im = _s5_discretize(s5_lambda_re[0], s5_lambda_im[0], s5_log_dt[0], s5_b_re[0], s5_b_im[0])
    cmat = jnp.concatenate([_block_diag_out(s5_c_re[0]), -_block_diag_out(s5_c_im[0])], axis=0).astype(BF16)
    s5_out = _s5(u.reshape(nb, s, S5_WIDTH), ab_re.reshape(S5_ROWS, LANES), ab_im.reshape(S5_ROWS, LANES),
                 _block_diag_in(bb_re).astype(BF16), _block_diag_in(bb_im).astype(BF16), cmat,
                 s5_d[0].reshape(-1), s5_w_glu[0].astype(BF16), s5_b_glu[0])

    o = _mla_attn(q.reshape(nb, s, -1), k.reshape(nb, s, -1), vt)
    w_out = ev_w_out[0].astype(BF16)
    xt = _ffn(xt, [(s5_out.reshape(t, S5_WIDTH), w_out[:S5_WIDTH]), (o.reshape(t, -1), w_out[S5_WIDTH:])],
              ffn_norm[0, 1], wgu[0, 1], wdn[0, 1])

    xt = _ffn(xt, [], ffn_norm[1, 0], wgu[1, 0], wdn[1, 0])
    cos_t, sin_t = _rope_table(positions, DIFF_ROT, ROPE_THETA, 0, DIFF_HEAD_DIM)
    width = DIFF_HEADS * 2 * DIFF_HEAD_DIM
    w_in = od_w_in[0]
    q, k, vt = _odd_prep(xt, od_norm[0], w_in[:, :2 * width].astype(BF16), w_in[:, 2 * width:].T.astype(BF16),
                         cos_t, sin_t, nb)
    lam_init = 0.8 - 0.6 * math.exp(-0.3 * 1)
    lam = (jnp.exp(jnp.sum(diff_lq1[0] * diff_lk1[0])) - jnp.exp(jnp.sum(diff_lq2[0] * diff_lk2[0])) + lam_init)
    o = _diff_attn(lam.reshape(1, 1), q.reshape(nb, s, -1), k.reshape(nb, s, -1), vt, diff_subln[0],
                   1.0 - lam_init)
    xt = _ffn(xt, [(o.reshape(t, -1), od_w_out[0].astype(BF16))], ffn_norm[1, 1], wgu[1, 1], wdn[1, 1],
              gfinal=final_norm)
    return xt.reshape(nb, s, d)
```

```python
import functools
import math

import jax
import jax.numpy as jnp
import numpy as np
from jax import lax
from jax.experimental import pallas as pl
from jax.experimental.pallas import tpu as pltpu

F32 = jnp.float32
BF16 = jnp.bfloat16

D_MODEL = 1024
NORM_EPS = 1e-6
D_FF = 2816
S5_WIDTH = 512
S5_GROUP_CH = 16
S5_GROUPS = 32
S5_STATE = 64
S5_FEATS = S5_GROUPS * S5_STATE
MLA_HEADS = 8
MLA_NOPE = 64
MLA_ROPE = 32
MLA_V = 64
MLA_Q_LORA = 384
MLA_KV_LORA = 256
MLA_ROPE_THETA = 10000.0
DIFF_HEADS = 8
DIFF_HEAD_DIM = 64
DIFF_ROT = 16
ROPE_THETA = 500000.0

LANES = 128
SUBLANES = 8
VMEM_LIMIT = 56 * 1024 * 1024
LOG2E = math.log2(math.e)

FFN_TM = 512
FFN_CHUNKS = 2
S5_TS = 256
S5_PITCH = S5_TS + 4
S5_ROWS = S5_FEATS // LANES
S5_TILE = 256
S5_EXPAND = S5_FEATS // S5_WIDTH
ATT_TQ = 512
ATT_TK = ATT_TQ // 2
ATT_LANES = 4 * ATT_TQ
ATT_STRIP = 256
ATT_SUM_ROWS = 16
ATT_PAD = LANES
PREP_TM = ATT_TQ

_NT = (((1,), (1,)), ((), ()))


def _rms(x, g):
    ms = jnp.mean(x * x, axis=-1, keepdims=True)
    return x * lax.rsqrt(ms + NORM_EPS) * g


def _const_spec(shape):
    nd = len(shape)
    return pl.BlockSpec(shape, lambda *_: (0,) * nd, pipeline_mode=pl.Buffered(1))


def _params(sem, flags=None):
    return pltpu.CompilerParams(dimension_semantics=sem, vmem_limit_bytes=VMEM_LIMIT, flags=flags)


def _ffn_body(*refs, n_mix, final):
    x_ref = refs[0]
    mix_refs = refs[1:1 + 2 * n_mix]
    g_ref, wgu_ref, wd_ref = refs[1 + 2 * n_mix:4 + 2 * n_mix]
    gf_ref = refs[4 + 2 * n_mix] if final else None
    o_ref = refs[-1]

    x = x_ref[...]
    for i in range(n_mix):
        x = x + jnp.dot(mix_refs[2 * i][...], mix_refs[2 * i + 1][...], preferred_element_type=F32)
    h = _rms(x, g_ref[...]).astype(BF16)
    ck = D_FF // FFN_CHUNKS
    y = None
    for c in range(FFN_CHUNKS):
        g = jnp.dot(h, wgu_ref[:, c * ck:(c + 1) * ck], preferred_element_type=F32)
        u = jnp.dot(h, wgu_ref[:, D_FF + c * ck:D_FF + (c + 1) * ck], preferred_element_type=F32)
        a = (g * jax.nn.sigmoid(g) * u).astype(BF16)
        d = jnp.dot(a, wd_ref[c * ck:(c + 1) * ck, :], preferred_element_type=F32)
        y = d if y is None else y + d
    out = x + 0.5 * y
    if final:
        out = _rms(out, gf_ref[...])
    o_ref[...] = out


def _ffn(x, mixes, g, wgu, wd, layer, pos, gfinal=None):
    t = x.shape[0]
    n_mix = len(mixes)
    final = gfinal is not None
    in_specs = [pl.BlockSpec((FFN_TM, D_MODEL), lambda i: (i, 0))]
    args = [x]
    for m, w in mixes:
        in_specs += [pl.BlockSpec((FFN_TM, m.shape[1]), lambda i: (i, 0)), _const_spec(w.shape)]
        args += [m, w]

    def slab(w):
        return pl.BlockSpec((None, None) + w.shape[2:], lambda i: (layer, pos, 0, 0), pipeline_mode=pl.Buffered(1))

    in_specs += [_const_spec((1, D_MODEL)), slab(wgu), slab(wd)]
    args += [g.reshape(1, D_MODEL), wgu, wd]
    if final:
        in_specs.append(_const_spec((1, D_MODEL)))
        args.append(gfinal.reshape(1, D_MODEL))
    return pl.pallas_call(
        functools.partial(_ffn_body, n_mix=n_mix, final=final),
        grid=(t // FFN_TM,),
        in_specs=in_specs,
        out_specs=pl.BlockSpec((FFN_TM, D_MODEL), lambda i: (i, 0)),
        out_shape=jax.ShapeDtypeStruct((t, D_MODEL), F32),
        compiler_params=_params(("parallel",)),
        name="ffn",
    )(*args)


def _rope(blk, cos, sin, half, first_lo, period):
    lane = lax.broadcasted_iota(jnp.int32, blk.shape, 1) % period
    fwd = pltpu.roll(blk, LANES - half, 1)
    bwd = pltpu.roll(blk, half, 1)
    rot = jnp.where(lane < first_lo + half, fwd, bwd)
    return blk * cos + rot * sin


def _even_prep_body(x_ref, g_ref, win_ref, qn_ref, wuq_ref, kvn_ref, wuk_ref, wuvt_ref, cos_ref, sin_ref,
                    u_ref, q_ref, k_ref, vt_ref):
    h = _rms(x_ref[...], g_ref[...]).astype(BF16)
    proj = jnp.dot(h, win_ref[...], preferred_element_type=F32)
    o_cq = S5_WIDTH
    o_ckv = o_cq + MLA_Q_LORA
    o_kr = o_ckv + MLA_KV_LORA
    u_ref[...] = proj[:, :o_cq]
    cq = proj[:, o_cq:o_ckv]
    ckv = proj[:, o_ckv:o_kr]
    kr = proj[:, o_kr:o_kr + LANES]
    cos = cos_ref[...]
    sin = sin_ref[...]
    half = MLA_ROPE // 2
    scale = (MLA_NOPE + MLA_ROPE) ** -0.5 * LOG2E
    q = jnp.dot(_rms(cq, qn_ref[...]).astype(BF16), wuq_ref[...], preferred_element_type=F32)
    ckn = _rms(ckv, kvn_ref[...]).astype(BF16)
    kn = jnp.dot(ckn, wuk_ref[...], preferred_element_type=F32)
    vt_ref[0, 0] = lax.dot_general(wuvt_ref[...], ckn, _NT, preferred_element_type=F32).astype(BF16)
    kpe = _rope(kr, cos, sin, half, MLA_NOPE, LANES)
    for hd in range(MLA_HEADS):
        sl = slice(hd * LANES, (hd + 1) * LANES)
        q_ref[:, sl] = (_rope(q[:, sl], cos, sin, half, MLA_NOPE, LANES) * scale).astype(BF16)
        k_ref[:, sl] = (kn[:, sl] + kpe).astype(BF16)


def _vt_spec(nblk, rows):
    return pl.BlockSpec((1, 1, rows, PREP_TM), lambda i: (i // nblk, i % nblk, 0, 0))


def _even_prep(x, g, win, qn, wuq, kvn, wuk, wuvt, cos, sin, nb):
    t = x.shape[0]
    tm = PREP_TM
    nblk = t // nb // tm
    row = lambda w: pl.BlockSpec((tm, w), lambda i: (i, 0))
    return pl.pallas_call(
        _even_prep_body,
        grid=(t // tm,),
        in_specs=[row(D_MODEL), _const_spec((1, D_MODEL)), _const_spec(win.shape),
                  _const_spec((1, MLA_Q_LORA)), _const_spec(wuq.shape),
                  _const_spec((1, MLA_KV_LORA)), _const_spec(wuk.shape), _const_spec(wuvt.shape),
                  row(LANES), row(LANES)],
        out_specs=[row(S5_WIDTH), row(MLA_HEADS * LANES), row(MLA_HEADS * LANES),
                   _vt_spec(nblk, MLA_HEADS * MLA_V)],
        out_shape=[jax.ShapeDtypeStruct((t, S5_WIDTH), F32),
                   jax.ShapeDtypeStruct((t, MLA_HEADS * LANES), BF16),
                   jax.ShapeDtypeStruct((t, MLA_HEADS * LANES), BF16),
                   jax.ShapeDtypeStruct((nb, nblk, MLA_HEADS * MLA_V, tm), BF16)],
        compiler_params=_params(("parallel",)),
        name="even_prep",
    )(x, g.reshape(1, -1), win, qn.reshape(1, -1), wuq, kvn.reshape(1, -1), wuk, wuvt, cos, sin)


def _odd_prep_body(x_ref, g_ref, wqk_ref, wvt_ref, cos_ref, sin_ref, q_ref, k_ref, vt_ref):
    h = _rms(x_ref[...], g_ref[...]).astype(BF16)
    cos = cos_ref[...]
    sin = sin_ref[...]
    half = DIFF_ROT // 2
    width = DIFF_HEADS * 2 * DIFF_HEAD_DIM
    scale = DIFF_HEAD_DIM ** -0.5 * LOG2E
    q = jnp.dot(h, wqk_ref[:, :width], preferred_element_type=F32)
    k = jnp.dot(h, wqk_ref[:, width:], preferred_element_type=F32)
    vt_ref[0, 0] = lax.dot_general(wvt_ref[...], h, _NT, preferred_element_type=F32).astype(BF16)
    for c in range(width // LANES):
        sl = slice(c * LANES, (c + 1) * LANES)
        q_ref[:, sl] = (_rope(q[:, sl], cos, sin, half, 0, DIFF_HEAD_DIM) * scale).astype(BF16)
        k_ref[:, sl] = _rope(k[:, sl], cos, sin, half, 0, DIFF_HEAD_DIM).astype(BF16)


def _odd_prep(x, g, wqk, wvt, cos, sin, nb):
    t = x.shape[0]
    tm = PREP_TM
    nblk = t // nb // tm
    width = DIFF_HEADS * 2 * DIFF_HEAD_DIM
    row = lambda w: pl.BlockSpec((tm, w), lambda i: (i, 0))
    return pl.pallas_call(
        _odd_prep_body,
        grid=(t // tm,),
        in_specs=[row(D_MODEL), _const_spec((1, D_MODEL)), _const_spec(wqk.shape), _const_spec(wvt.shape),
                  row(LANES), row(LANES)],
        out_specs=[row(width), row(width), _vt_spec(nblk, width)],
        out_shape=[jax.ShapeDtypeStruct((t, width), BF16), jax.ShapeDtypeStruct((t, width), BF16),
                   jax.ShapeDtypeStruct((nb, nblk, width, tm), BF16)],
        compiler_params=_params(("parallel",)),
        name="odd_prep",
    )(x, g.reshape(1, -1), wqk, wvt, cos, sin)


def _s5_body(u_ref, ar_ref, ai_ref, bre_ref, bim_ref, c_ref, d_ref, wglu_ref, bglu_ref, o_ref,
             xr_ref, xi_ref, sr_ref, si_ref):
    nb = u_ref.shape[0]

    @pl.when(pl.program_id(0) == 0)
    def _():
        sr_ref[...] = jnp.zeros_like(sr_ref)
        si_ref[...] = jnp.zeros_like(si_ref)

    for b in range(nb):
        ub = u_ref[b].astype(BF16)
        for n in range(S5_FEATS // S5_TILE):
            k = n * S5_TILE // S5_EXPAND // S5_TILE
            uk = ub[:, k * S5_TILE:(k + 1) * S5_TILE]
            bur = jnp.dot(uk, bre_ref[n], preferred_element_type=F32)
            bui = jnp.dot(uk, bim_ref[n], preferred_element_type=F32)
            for h in range(S5_TILE // LANES):
                j = n * (S5_TILE // LANES) + h
                xr_ref[b, j * S5_PITCH:j * S5_PITCH + S5_TS, :] = bur[:, h * LANES:(h + 1) * LANES]
                xi_ref[b, j * S5_PITCH:j * S5_PITCH + S5_TS, :] = bui[:, h * LANES:(h + 1) * LANES]

    ar = ar_ref[...]
    ai = ai_ref[...]

    def step(t, carry):
        new = []
        for b in range(nb):
            pr, pi = carry[2 * b], carry[2 * b + 1]
            idx = pl.ds(t, S5_ROWS, stride=S5_PITCH)
            nr = ar * pr - ai * pi + xr_ref[b, idx, :]
            ni = ar * pi + ai * pr + xi_ref[b, idx, :]
            xr_ref[b, idx, :] = nr
            xi_ref[b, idx, :] = ni
            new += [nr, ni]
        return tuple(new)

    init = []
    for b in range(nb):
        init += [sr_ref[b], si_ref[b]]
    fin = lax.fori_loop(0, S5_TS, step, tuple(init), unroll=4)
    for b in range(nb):
        sr_ref[b] = fin[2 * b]
        si_ref[b] = fin[2 * b + 1]

    per_tile = S5_ROWS * S5_TILE // S5_WIDTH
    for b in range(nb):
        ys = []
        for k in range(S5_WIDTH // S5_TILE):
            slabs = range(k * per_tile, (k + 1) * per_tile)
            parts = [xr_ref[b, j * S5_PITCH:j * S5_PITCH + S5_TS, :].astype(BF16) for j in slabs]
            parts += [xi_ref[b, j * S5_PITCH:j * S5_PITCH + S5_TS, :].astype(BF16) for j in slabs]
            ys.append(jnp.dot(jnp.concatenate(parts, axis=1), c_ref[k], preferred_element_type=F32))
        y = jnp.concatenate(ys, axis=1) + d_ref[...] * u_ref[b]
        g = 0.5 * y * (1.0 + jnp.tanh(math.sqrt(2.0 / math.pi) * (y + 0.044715 * (y * y * y))))
        z = jnp.dot(g.astype(BF16), wglu_ref[...], preferred_element_type=F32) + bglu_ref[...]
        o_ref[b] = (g * jax.nn.sigmoid(z)).astype(BF16)


def _s5(u, ar, ai, bre, bim, cmat, dskip, wglu, bglu):
    nb, s, _ = u.shape
    blk = pl.BlockSpec((nb, S5_TS, S5_WIDTH), lambda i: (0, i, 0))
    return pl.pallas_call(
        _s5_body,
        grid=(s // S5_TS,),
        in_specs=[blk, _const_spec(ar.shape), _const_spec(ai.shape), _const_spec(bre.shape),
                  _const_spec(bim.shape), _const_spec(cmat.shape), _const_spec((1, S5_WIDTH)),
                  _const_spec(wglu.shape), _const_spec((1, S5_WIDTH))],
        out_specs=blk,
        out_shape=jax.ShapeDtypeStruct((nb, s, S5_WIDTH), BF16),
        scratch_shapes=[pltpu.VMEM((nb, S5_ROWS * S5_PITCH, LANES), F32),
                        pltpu.VMEM((nb, S5_ROWS * S5_PITCH, LANES), F32),
                        pltpu.VMEM((nb, S5_ROWS, LANES), F32),
                        pltpu.VMEM((nb, S5_ROWS, LANES), F32)],
        compiler_params=_params(("arbitrary",)),
        name="s5",
    )(u, ar, ai, bre, bim, cmat, dskip.reshape(1, -1), wglu, bglu.reshape(1, -1))


def _flash_t(k_ref, vt_ref, qt_ref, qi, scr, k_block, v_block):
    s0, s1, p0, p1, a0, a1, x0, x1, m_ref, acc_ref = scr
    strips = tuple(range(ATT_LANES // ATT_STRIP))

    def q_off(c):
        return (c * ATT_STRIP) % ATT_TQ

    lower = tuple(c for c in strips if q_off(c) < ATT_TK)
    upper = tuple(c for c in strips if q_off(c) >= ATT_TK)
    m_ref[...] = jnp.full(m_ref.shape, -jnp.inf, F32)
    acc_ref[...] = jnp.zeros(acc_ref.shape, F32)
    p1[...] = jnp.zeros(p1.shape, BF16)
    a1[...] = jnp.ones(a1.shape, F32)
    ones = jnp.ones((ATT_SUM_ROWS, ATT_TK), BF16)

    def scores(j, s_ref, x_ref, which):
        for c in which:
            sl = slice(c * ATT_STRIP, (c + 1) * ATT_STRIP)
            kb = k_ref[0, pl.ds(j * ATT_TK, ATT_TK), k_block(c) * LANES:(k_block(c) + 1) * LANES]
            s = jnp.dot(kb, qt_ref[:, sl], preferred_element_type=F32)
            s_ref[:, sl] = s
            x_ref[:, sl] = jnp.max(s, axis=0, keepdims=True)

    def softmax(s_ref, x_ref, p_ref, a_ref, which, key_off):
        for c in which:
            for t in range(ATT_STRIP // LANES):
                sl = slice(c * ATT_STRIP + t * LANES, c * ATT_STRIP + (t + 1) * LANES)
                s = s_ref[:, sl]
                if key_off is None:
                    s_max = x_ref[:, sl]
                else:
                    kpos = lax.broadcasted_iota(jnp.int32, s.shape, 0) + key_off
                    qpos = lax.broadcasted_iota(jnp.int32, s.shape, 1) + (q_off(c) + t * LANES)
                    s = jnp.where(kpos <= qpos, s, -jnp.inf)
                    s_max = jnp.max(s, axis=0, keepdims=True)
                m_old = m_ref[:, sl]
                m_new = jnp.maximum(m_old, s_max)
                a_ref[:, sl] = jnp.exp2(m_old - m_new)
                m_ref[:, sl] = m_new
                p_ref[:, sl] = jnp.exp2(s - m_new).astype(BF16)

    def value(pair, half, p_ref, a_ref, which):
        for c in which:
            sl = slice(c * ATT_STRIP, (c + 1) * ATT_STRIP)
            vt = vt_ref[0, pair, v_block(c) * LANES:(v_block(c) + 1) * LANES, half * ATT_TK:(half + 1) * ATT_TK]
            pv = jnp.dot(jnp.concatenate([vt, ones], axis=0), p_ref[:, sl], preferred_element_type=F32)
            acc_ref[:, sl] = a_ref[:, sl] * acc_ref[:, sl] + pv

    scores(0, s0, x0, strips)

    def pair_step(i, carry):
        softmax(s0, x0, p0, a0, strips, None)
        scores(2 * i + 1, s1, x1, strips)
        value(jnp.maximum(i - 1, 0), 1, p1, a1, strips)
        softmax(s1, x1, p1, a1, strips, None)
        scores(2 * i + 2, s0, x0, strips)
        value(i, 0, p0, a0, strips)
        return carry

    lax.fori_loop(0, qi, pair_step, 0)
    scores(2 * qi + 1, s1, x1, upper)
    softmax(s0, x0, p0, a0, lower, 0)
    softmax(s0, x0, p0, a0, upper, None)
    value(jnp.maximum(qi - 1, 0), 1, p1, a1, strips)
    softmax(s1, x1, p1, a1, upper, ATT_TK)
    value(qi, 0, p0, a0, strips)
    value(qi, 1, p1, a1, upper)
    return acc_ref[:LANES, :ATT_LANES] / acc_ref[LANES:LANES + 1, :ATT_LANES]


def _mla_attn_body(q_ref, k_ref, vt_ref, o_ref, *scr):
    qi = pl.program_id(2)
    qt_ref = scr[0]
    heads = ATT_LANES // ATT_TQ
    for h in range(heads):
        qt_ref[:, h * ATT_TQ:(h + 1) * ATT_TQ] = q_ref[0, :, h * LANES:(h + 1) * LANES].astype(F32).T.astype(BF16)
    per_head = ATT_TQ // ATT_STRIP
    ot = _flash_t(k_ref, vt_ref, qt_ref, qi, scr[1:], lambda c: c // per_head, lambda c: c // (2 * per_head))
    row = lax.broadcasted_iota(jnp.int32, (LANES, ATT_TQ), 0)
    for r in range(heads // 2):
        lo = ot[:, 2 * r * ATT_TQ:(2 * r + 1) * ATT_TQ]
        hi = ot[:, (2 * r + 1) * ATT_TQ:(2 * r + 2) * ATT_TQ]
        o_ref[0, :, r * LANES:(r + 1) * LANES] = jnp.where(row < MLA_V, lo, hi).T.astype(BF16)


def _attn_specs(s, qk_width, v_width):
    n = ATT_LANES
    row = pltpu.VMEM((1, n), F32)
    return dict(
        in_specs=[pl.BlockSpec((1, ATT_TQ, qk_width), lambda b, h, i: (b, i, h)),
                  pl.BlockSpec((1, s, qk_width), lambda b, h, i: (b, 0, h)),
                  pl.BlockSpec((1, s // ATT_TQ, v_width, ATT_TQ), lambda b, h, i: (b, 0, h, 0))],
        out_specs=pl.BlockSpec((1, ATT_TQ, v_width), lambda b, h, i: (b, i, h)),
        scratch_shapes=[pltpu.VMEM((LANES, n), BF16),
                        pltpu.VMEM((ATT_TK, n + ATT_PAD), F32), pltpu.VMEM((ATT_TK, n + ATT_PAD), F32),
                        pltpu.VMEM((ATT_TK, n), BF16), pltpu.VMEM((ATT_TK, n), BF16),
                        row, row, row, row, row, pltpu.VMEM((LANES + ATT_SUM_ROWS, n + ATT_PAD), F32)],
        compiler_params=_params(("parallel", "parallel", "arbitrary")))


def _mla_attn(q, k, vt):
    nb, s, _ = q.shape
    heads = ATT_LANES // ATT_TQ
    return pl.pallas_call(
        _mla_attn_body,
        grid=(nb, MLA_HEADS // heads, s // ATT_TQ),
        out_shape=jax.ShapeDtypeStruct((nb, s, MLA_HEADS * MLA_V), BF16),
        name="mla_attn", **_attn_specs(s, heads * LANES, heads * MLA_V),
    )(q, k, vt)


def _diff_attn_body(lam_ref, q_ref, k_ref, vt_ref, sub_ref, o_ref, *scr, out_scale):
    qi = pl.program_id(2)
    qt_ref = scr[0]
    heads = ATT_LANES // (2 * ATT_TQ)
    for g in range(heads):
        qt = q_ref[0, :, g * LANES:(g + 1) * LANES].astype(F32).T
        row = lax.broadcasted_iota(jnp.int32, qt.shape, 0)
        zero = jnp.zeros_like(qt)
        qt_ref[:, 2 * g * ATT_TQ:(2 * g + 1) * ATT_TQ] = jnp.where(row < DIFF_HEAD_DIM, qt, zero).astype(BF16)
        qt_ref[:, (2 * g + 1) * ATT_TQ:(2 * g + 2) * ATT_TQ] = jnp.where(row < DIFF_HEAD_DIM, zero, qt).astype(BF16)
    per_head = 2 * ATT_TQ // ATT_STRIP
    ot = _flash_t(k_ref, vt_ref, qt_ref, qi, scr[1:], lambda c: c // per_head, lambda c: c // per_head)
    for g in range(heads):
        o = (ot[:, 2 * g * ATT_TQ:(2 * g + 1) * ATT_TQ]
             - lam_ref[0, 0] * ot[:, (2 * g + 1) * ATT_TQ:(2 * g + 2) * ATT_TQ]).T
        o_ref[0, :, g * LANES:(g + 1) * LANES] = (_rms(o, sub_ref[...]) * out_scale).astype(BF16)


def _diff_attn(lam, q, k, vt, subln, out_scale):
    nb, s, _ = q.shape
    heads = ATT_LANES // (2 * ATT_TQ)
    spec = _attn_specs(s, heads * LANES, heads * LANES)
    spec["in_specs"] = ([pl.BlockSpec(memory_space=pltpu.SMEM)] + spec["in_specs"]
                        + [pl.BlockSpec((1, LANES), lambda b, h, i: (0, 0))])
    return pl.pallas_call(
        functools.partial(_diff_attn_body, out_scale=out_scale),
        grid=(nb, DIFF_HEADS // heads, s // ATT_TQ),
        out_shape=jax.ShapeDtypeStruct((nb, s, DIFF_HEADS * 2 * DIFF_HEAD_DIM), BF16),
        name="diff_attn", **spec,
    )(lam, q, k, vt, subln.reshape(1, -1))


def _rope_table(positions, rot_dim, theta, first_lo, period):
    half = rot_dim // 2
    inv = theta ** (-jnp.arange(half, dtype=F32) * 2.0 / rot_dim)
    rel = np.arange(LANES) % period - first_lo
    rotated = (rel >= 0) & (rel < rot_dim)
    inv_lane = jnp.where(rotated, inv[np.where(rotated, rel % half, 0)], 0.0)
    sign = np.where(rotated, np.where(rel < half, -1.0, 1.0), 0.0).astype(np.float32)
    ang = positions.astype(F32).reshape(-1, 1) * inv_lane
    return jnp.cos(ang), jnp.sin(ang) * sign


def _s5_discretize(lam_re, lam_im, log_dt, b_re, b_im):
    dt = jnp.exp(log_dt)[:, None]
    mag = jnp.exp(lam_re * dt)
    ang = lam_im * dt
    ab_re = mag * jnp.cos(ang)
    ab_im = mag * jnp.sin(ang)
    den = lam_re * lam_re + lam_im * lam_im
    nr = ab_re - 1.0
    f_re = (nr * lam_re + ab_im * lam_im) / den
    f_im = (ab_im * lam_re - nr * lam_im) / den
    bb_re = f_re[..., None] * b_re - f_im[..., None] * b_im
    bb_im = f_re[..., None] * b_im + f_im[..., None] * b_re
    return ab_re, ab_im, bb_re, bb_im


def _block_diag_in(bb):
    eye = jnp.eye(S5_GROUPS, dtype=F32)
    return jnp.einsum('gph,gk->ghkp', bb, eye).reshape(S5_WIDTH, S5_FEATS)


def _block_diag_out(c):
    eye = jnp.eye(S5_GROUPS, dtype=F32)
    return jnp.einsum('ghp,gk->gpkh', c, eye).reshape(S5_FEATS, S5_WIDTH)


def _s5_in_tiles(bb):
    full = _block_diag_in(bb)
    tiles = [full[(n * S5_TILE // S5_EXPAND // S5_TILE) * S5_TILE:, n * S5_TILE:(n + 1) * S5_TILE][:S5_TILE]
             for n in range(S5_FEATS // S5_TILE)]
    return jnp.stack(tiles).astype(BF16)


def _s5_out_tiles(c_re, c_im):
    full_re, full_im = _block_diag_out(c_re), -_block_diag_out(c_im)
    rows = S5_TILE * S5_EXPAND
    tiles = [jnp.concatenate([full_re[k * rows:(k + 1) * rows, k * S5_TILE:(k + 1) * S5_TILE],
                              full_im[k * rows:(k + 1) * rows, k * S5_TILE:(k + 1) * S5_TILE]], axis=0)
             for k in range(S5_WIDTH // S5_TILE)]
    return jnp.stack(tiles).astype(BF16)


def _pad_heads(w, heads, width, lo, hi):
    kdim = w.shape[0]
    w = w.reshape(kdim, heads, width)[:, :, lo:hi]
    w = jnp.pad(w, ((0, 0), (0, 0), (0, LANES - (hi - lo))))
    return w.reshape(kdim, heads * LANES)


def kernel(x, positions, ffn_norm, ffn_w_gu, ffn_w_down, ev_norm, ev_w_in, s5_lambda_re, s5_lambda_im, s5_log_dt, s5_b_re, s5_b_im, s5_c_re, s5_c_im, s5_d, s5_w_glu, s5_b_glu, mla_q_norm, mla_w_uq, mla_kv_norm, mla_w_ukv, ev_w_out, od_norm, od_w_in, diff_lq1, diff_lk1, diff_lq2, diff_lk2, diff_subln, od_w_out, final_norm):
    nb, s, d = x.shape
    t = nb * s
    xt = x.reshape(t, d)
    wgu = ffn_w_gu.astype(BF16)
    wdn = ffn_w_down.astype(BF16)

    xt = _ffn(xt, [], ffn_norm[0, 0], wgu, wdn, 0, 0)

    cos_m, sin_m = _rope_table(positions, MLA_ROPE, MLA_ROPE_THETA, MLA_NOPE, LANES)
    w_in = ev_w_in[0]
    o_kr = S5_WIDTH + MLA_Q_LORA + MLA_KV_LORA
    w_kr = jnp.pad(w_in[:, o_kr:], ((0, 0), (MLA_NOPE, LANES - MLA_NOPE - MLA_ROPE)))
    win_p = jnp.concatenate([w_in[:, :o_kr], w_kr], axis=1).astype(BF16)
    wuq_p = _pad_heads(mla_w_uq[0], MLA_HEADS, MLA_NOPE + MLA_ROPE, 0, MLA_NOPE + MLA_ROPE).astype(BF16)
    wuk_p = _pad_heads(mla_w_ukv[0], MLA_HEADS, MLA_NOPE + MLA_V, 0, MLA_NOPE).astype(BF16)
    wuv = mla_w_ukv[0].reshape(MLA_KV_LORA, MLA_HEADS, MLA_NOPE + MLA_V)[:, :, MLA_NOPE:]
    wuvt = wuv.reshape(MLA_KV_LORA, MLA_HEADS * MLA_V).T.astype(BF16)
    u, q, k, vt = _even_prep(xt, ev_norm[0], win_p, mla_q_norm[0], wuq_p, mla_kv_norm[0], wuk_p, wuvt,
                             cos_m, sin_m, nb)

    ab_re, ab_im, bb_re, bb_im = _s5_discretize(s5_lambda_re[0], s5_lambda_im[0], s5_log_dt[0], s5_b_re[0], s5_b_im[0])
    s5_out = _s5(u.reshape(nb, s, S5_WIDTH), ab_re.reshape(S5_ROWS, LANES), ab_im.reshape(S5_ROWS, LANES),
                 _s5_in_tiles(bb_re), _s5_in_tiles(bb_im), _s5_out_tiles(s5_c_re[0], s5_c_im[0]),
                 s5_d[0].reshape(-1), s5_w_glu[0].astype(BF16), s5_b_glu[0])

    o = _mla_attn(q.reshape(nb, s, -1), k.reshape(nb, s, -1), vt)
    w_out = ev_w_out[0].astype(BF16)
    xt = _ffn(xt, [(s5_out.reshape(t, S5_WIDTH), w_out[:S5_WIDTH]), (o.reshape(t, -1), w_out[S5_WIDTH:])],
              ffn_norm[0, 1], wgu, wdn, 0, 1)

    xt = _ffn(xt, [], ffn_norm[1, 0], wgu, wdn, 1, 0)
    cos_t, sin_t = _rope_table(positions, DIFF_ROT, ROPE_THETA, 0, DIFF_HEAD_DIM)
    width = DIFF_HEADS * 2 * DIFF_HEAD_DIM
    w_in = od_w_in[0]
    q, k, vt = _odd_prep(xt, od_norm[0], w_in[:, :2 * width].astype(BF16), w_in[:, 2 * width:].T.astype(BF16),
                         cos_t, sin_t, nb)
    lam_init = 0.8 - 0.6 * math.exp(-0.3 * 1)
    lam = (jnp.exp(jnp.sum(diff_lq1[0] * diff_lk1[0])) - jnp.exp(jnp.sum(diff_lq2[0] * diff_lk2[0])) + lam_init)
    o = _diff_attn(lam.reshape(1, 1), q.reshape(nb, s, -1), k.reshape(nb, s, -1), vt, diff_subln[0],
                   1.0 - lam_init)
    xt = _ffn(xt, [(o.reshape(t, -1), od_w_out[0].astype(BF16))], ffn_norm[1, 1], wgu, wdn, 1, 1,
              gfinal=final_norm)
    return xt.reshape(nb, s, d)
```

```python
import functools
import math

import jax
import jax.numpy as jnp
import numpy as np
from jax import lax
from jax.experimental import pallas as pl
from jax.experimental.pallas import tpu as pltpu

F32 = jnp.float32
BF16 = jnp.bfloat16

D_MODEL = 1024
NORM_EPS = 1e-6
D_FF = 2816
S5_WIDTH = 512
S5_GROUP_CH = 16
S5_GROUPS = 32
S5_STATE = 64
S5_FEATS = S5_GROUPS * S5_STATE
MLA_HEADS = 8
MLA_NOPE = 64
MLA_ROPE = 32
MLA_V = 64
MLA_Q_LORA = 384
MLA_KV_LORA = 256
MLA_ROPE_THETA = 10000.0
DIFF_HEADS = 8
DIFF_HEAD_DIM = 64
DIFF_ROT = 16
ROPE_THETA = 500000.0

LANES = 128
SUBLANES = 8
VMEM_LIMIT = 56 * 1024 * 1024
LOG2E = math.log2(math.e)

FFN_TM = 512
FFN_CHUNKS = 1
S5_TS = 256
S5_PITCH = S5_TS + 4
S5_ROWS = S5_FEATS // LANES
S5_TILE = 256
S5_EXPAND = S5_FEATS // S5_WIDTH
ATT_TQ = 512
ATT_TK = ATT_TQ // 2
ATT_LANES = 4 * ATT_TQ
ATT_STRIP = 256
ATT_SUM_ROWS = 16
ATT_PAD = LANES
PREP_TM = ATT_TQ

_NT = (((1,), (1,)), ((), ()))


def _rms(x, g):
    ms = jnp.mean(x * x, axis=-1, keepdims=True)
    return x * lax.rsqrt(ms + NORM_EPS) * g


def _const_spec(shape):
    nd = len(shape)
    return pl.BlockSpec(shape, lambda *_: (0,) * nd, pipeline_mode=pl.Buffered(1))


def _params(sem, flags=None):
    return pltpu.CompilerParams(dimension_semantics=sem, vmem_limit_bytes=VMEM_LIMIT, flags=flags)


def _ffn_body(*refs, n_mix, final):
    x_ref = refs[0]
    mix_refs = refs[1:1 + 2 * n_mix]
    g_ref, wgu_ref, wd_ref = refs[1 + 2 * n_mix:4 + 2 * n_mix]
    gf_ref = refs[4 + 2 * n_mix] if final else None
    o_ref = refs[-1]

    x = x_ref[...]
    for i in range(n_mix):
        x = x + jnp.dot(mix_refs[2 * i][...], mix_refs[2 * i + 1][...], preferred_element_type=F32)
    h = _rms(x, g_ref[...]).astype(BF16)
    ck = D_FF // FFN_CHUNKS
    y = None
    for c in range(FFN_CHUNKS):
        g = jnp.dot(h, wgu_ref[:, c * ck:(c + 1) * ck], preferred_element_type=F32)
        u = jnp.dot(h, wgu_ref[:, D_FF + c * ck:D_FF + (c + 1) * ck], preferred_element_type=F32)
        a = (g * jax.nn.sigmoid(g) * u).astype(BF16)
        d = jnp.dot(a, wd_ref[c * ck:(c + 1) * ck, :], preferred_element_type=F32)
        y = d if y is None else y + d
    out = x + 0.5 * y
    if final:
        out = _rms(out, gf_ref[...])
    o_ref[...] = out


def _ffn(x, mixes, g, wgu, wd, layer, pos, gfinal=None):
    t = x.shape[0]
    n_mix = len(mixes)
    final = gfinal is not None
    in_specs = [pl.BlockSpec((FFN_TM, D_MODEL), lambda i: (i, 0))]
    args = [x]
    for m, w in mixes:
        in_specs += [pl.BlockSpec((FFN_TM, m.shape[1]), lambda i: (i, 0)), _const_spec(w.shape)]
        args += [m, w]

    def slab(w):
        return pl.BlockSpec((None, None) + w.shape[2:], lambda i: (layer, pos, 0, 0), pipeline_mode=pl.Buffered(1))

    in_specs += [_const_spec((1, D_MODEL)), slab(wgu), slab(wd)]
    args += [g.reshape(1, D_MODEL), wgu, wd]
    if final:
        in_specs.append(_const_spec((1, D_MODEL)))
        args.append(gfinal.reshape(1, D_MODEL))
    return pl.pallas_call(
        functools.partial(_ffn_body, n_mix=n_mix, final=final),
        grid=(t // FFN_TM,),
        in_specs=in_specs,
        out_specs=pl.BlockSpec((FFN_TM, D_MODEL), lambda i: (i, 0)),
        out_shape=jax.ShapeDtypeStruct((t, D_MODEL), F32),
        compiler_params=_params(("parallel",)),
        name="ffn",
    )(*args)


def _rope(blk, cos, sin, half, first_lo, period):
    lane = lax.broadcasted_iota(jnp.int32, blk.shape, 1) % period
    fwd = pltpu.roll(blk, LANES - half, 1)
    bwd = pltpu.roll(blk, half, 1)
    rot = jnp.where(lane < first_lo + half, fwd, bwd)
    return blk * cos + rot * sin


def _even_prep_body(x_ref, g_ref, win_ref, qn_ref, wuq_ref, kvn_ref, wuk_ref, wuvt_ref, cos_ref, sin_ref,
                    u_ref, q_ref, k_ref, vt_ref):
    h = _rms(x_ref[...], g_ref[...]).astype(BF16)
    proj = jnp.dot(h, win_ref[...], preferred_element_type=F32)
    o_cq = S5_WIDTH
    o_ckv = o_cq + MLA_Q_LORA
    o_kr = o_ckv + MLA_KV_LORA
    u_ref[...] = proj[:, :o_cq]
    cq = proj[:, o_cq:o_ckv]
    ckv = proj[:, o_ckv:o_kr]
    kr = proj[:, o_kr:o_kr + LANES]
    cos = cos_ref[...]
    sin = sin_ref[...]
    half = MLA_ROPE // 2
    scale = (MLA_NOPE + MLA_ROPE) ** -0.5 * LOG2E
    q = jnp.dot(_rms(cq, qn_ref[...]).astype(BF16), wuq_ref[...], preferred_element_type=F32)
    ckn = _rms(ckv, kvn_ref[...]).astype(BF16)
    kn = jnp.dot(ckn, wuk_ref[...], preferred_element_type=F32)
    vt_ref[0, 0] = lax.dot_general(wuvt_ref[...], ckn, _NT, preferred_element_type=F32).astype(BF16)
    kpe = _rope(kr, cos, sin, half, MLA_NOPE, LANES)
    for hd in range(MLA_HEADS):
        sl = slice(hd * LANES, (hd + 1) * LANES)
        q_ref[:, sl] = (_rope(q[:, sl], cos, sin, half, MLA_NOPE, LANES) * scale).astype(BF16)
        k_ref[:, sl] = (kn[:, sl] + kpe).astype(BF16)


def _vt_spec(nblk, rows):
    return pl.BlockSpec((1, 1, rows, PREP_TM), lambda i: (i // nblk, i % nblk, 0, 0))


def _even_prep(x, g, win, qn, wuq, kvn, wuk, wuvt, cos, sin, nb):
    t = x.shape[0]
    tm = PREP_TM
    nblk = t // nb // tm
    row = lambda w: pl.BlockSpec((tm, w), lambda i: (i, 0))
    return pl.pallas_call(
        _even_prep_body,
        grid=(t // tm,),
        in_specs=[row(D_MODEL), _const_spec((1, D_MODEL)), _const_spec(win.shape),
                  _const_spec((1, MLA_Q_LORA)), _const_spec(wuq.shape),
                  _const_spec((1, MLA_KV_LORA)), _const_spec(wuk.shape), _const_spec(wuvt.shape),
                  row(LANES), row(LANES)],
        out_specs=[row(S5_WIDTH), row(MLA_HEADS * LANES), row(MLA_HEADS * LANES),
                   _vt_spec(nblk, MLA_HEADS * MLA_V)],
        out_shape=[jax.ShapeDtypeStruct((t, S5_WIDTH), F32),
                   jax.ShapeDtypeStruct((t, MLA_HEADS * LANES), BF16),
                   jax.ShapeDtypeStruct((t, MLA_HEADS * LANES), BF16),
                   jax.ShapeDtypeStruct((nb, nblk, MLA_HEADS * MLA_V, tm), BF16)],
        compiler_params=_params(("parallel",)),
        name="even_prep",
    )(x, g.reshape(1, -1), win, qn.reshape(1, -1), wuq, kvn.reshape(1, -1), wuk, wuvt, cos, sin)


def _odd_prep_body(x_ref, g_ref, wqk_ref, wvt_ref, cos_ref, sin_ref, q_ref, k_ref, vt_ref):
    h = _rms(x_ref[...], g_ref[...]).astype(BF16)
    cos = cos_ref[...]
    sin = sin_ref[...]
    half = DIFF_ROT // 2
    width = DIFF_HEADS * 2 * DIFF_HEAD_DIM
    scale = DIFF_HEAD_DIM ** -0.5 * LOG2E
    q = jnp.dot(h, wqk_ref[:, :width], preferred_element_type=F32)
    k = jnp.dot(h, wqk_ref[:, width:], preferred_element_type=F32)
    vt_ref[0, 0] = lax.dot_general(wvt_ref[...], h, _NT, preferred_element_type=F32).astype(BF16)
    for c in range(width // LANES):
        sl = slice(c * LANES, (c + 1) * LANES)
        q_ref[:, sl] = (_rope(q[:, sl], cos, sin, half, 0, DIFF_HEAD_DIM) * scale).astype(BF16)
        k_ref[:, sl] = _rope(k[:, sl], cos, sin, half, 0, DIFF_HEAD_DIM).astype(BF16)


def _odd_prep(x, g, wqk, wvt, cos, sin, nb):
    t = x.shape[0]
    tm = PREP_TM
    nblk = t // nb // tm
    width = DIFF_HEADS * 2 * DIFF_HEAD_DIM
    row = lambda w: pl.BlockSpec((tm, w), lambda i: (i, 0))
    return pl.pallas_call(
        _odd_prep_body,
        grid=(t // tm,),
        in_specs=[row(D_MODEL), _const_spec((1, D_MODEL)), _const_spec(wqk.shape), _const_spec(wvt.shape),
                  row(LANES), row(LANES)],
        out_specs=[row(width), row(width), _vt_spec(nblk, width)],
        out_shape=[jax.ShapeDtypeStruct((t, width), BF16), jax.ShapeDtypeStruct((t, width), BF16),
                   jax.ShapeDtypeStruct((nb, nblk, width, tm), BF16)],
        compiler_params=_params(("parallel",)),
        name="odd_prep",
    )(x, g.reshape(1, -1), wqk, wvt, cos, sin)


def _s5_body(u_ref, ar_ref, ai_ref, bre_ref, bim_ref, c_ref, d_ref, wglu_ref, bglu_ref, o_ref,
             xr_ref, xi_ref, sr_ref, si_ref):
    nb = u_ref.shape[0]

    @pl.when(pl.program_id(0) == 0)
    def _():
        sr_ref[...] = jnp.zeros_like(sr_ref)
        si_ref[...] = jnp.zeros_like(si_ref)

    for b in range(nb):
        ub = u_ref[b].astype(BF16)
        for n in range(S5_FEATS // S5_TILE):
            k = n * S5_TILE // S5_EXPAND // S5_TILE
            uk = ub[:, k * S5_TILE:(k + 1) * S5_TILE]
            bur = jnp.dot(uk, bre_ref[n], preferred_element_type=F32)
            bui = jnp.dot(uk, bim_ref[n], preferred_element_type=F32)
            for h in range(S5_TILE // LANES):
                j = n * (S5_TILE // LANES) + h
                xr_ref[b, j * S5_PITCH:j * S5_PITCH + S5_TS, :] = bur[:, h * LANES:(h + 1) * LANES]
                xi_ref[b, j * S5_PITCH:j * S5_PITCH + S5_TS, :] = bui[:, h * LANES:(h + 1) * LANES]

    ar = ar_ref[...]
    ai = ai_ref[...]

    def step(t, carry):
        new = []
        for b in range(nb):
            pr, pi = carry[2 * b], carry[2 * b + 1]
            idx = pl.ds(t, S5_ROWS, stride=S5_PITCH)
            nr = ar * pr - ai * pi + xr_ref[b, idx, :]
            ni = ar * pi + ai * pr + xi_ref[b, idx, :]
            xr_ref[b, idx, :] = nr
            xi_ref[b, idx, :] = ni
            new += [nr, ni]
        return tuple(new)

    init = []
    for b in range(nb):
        init += [sr_ref[b], si_ref[b]]
    fin = lax.fori_loop(0, S5_TS, step, tuple(init), unroll=4)
    for b in range(nb):
        sr_ref[b] = fin[2 * b]
        si_ref[b] = fin[2 * b + 1]

    per_tile = S5_ROWS * S5_TILE // S5_WIDTH
    for b in range(nb):
        ys = []
        for k in range(S5_WIDTH // S5_TILE):
            slabs = range(k * per_tile, (k + 1) * per_tile)
            parts = [xr_ref[b, j * S5_PITCH:j * S5_PITCH + S5_TS, :].astype(BF16) for j in slabs]
            parts += [xi_ref[b, j * S5_PITCH:j * S5_PITCH + S5_TS, :].astype(BF16) for j in slabs]
            ys.append(jnp.dot(jnp.concatenate(parts, axis=1), c_ref[k], preferred_element_type=F32))
        y = jnp.concatenate(ys, axis=1) + d_ref[...] * u_ref[b]
        g = 0.5 * y * (1.0 + jnp.tanh(math.sqrt(2.0 / math.pi) * (y + 0.044715 * (y * y * y))))
        z = jnp.dot(g.astype(BF16), wglu_ref[...], preferred_element_type=F32) + bglu_ref[...]
        o_ref[b] = (g * jax.nn.sigmoid(z)).astype(BF16)


def _s5(u, ar, ai, bre, bim, cmat, dskip, wglu, bglu):
    nb, s, _ = u.shape
    blk = pl.BlockSpec((nb, S5_TS, S5_WIDTH), lambda i: (0, i, 0))
    return pl.pallas_call(
        _s5_body,
        grid=(s // S5_TS,),
        in_specs=[blk, _const_spec(ar.shape), _const_spec(ai.shape), _const_spec(bre.shape),
                  _const_spec(bim.shape), _const_spec(cmat.shape), _const_spec((1, S5_WIDTH)),
                  _const_spec(wglu.shape), _const_spec((1, S5_WIDTH))],
        out_specs=blk,
        out_shape=jax.ShapeDtypeStruct((nb, s, S5_WIDTH), BF16),
        scratch_shapes=[pltpu.VMEM((nb, S5_ROWS * S5_PITCH, LANES), F32),
                        pltpu.VMEM((nb, S5_ROWS * S5_PITCH, LANES), F32),
                        pltpu.VMEM((nb, S5_ROWS, LANES), F32),
                        pltpu.VMEM((nb, S5_ROWS, LANES), F32)],
        compiler_params=_params(("arbitrary",)),
        name="s5",
    )(u, ar, ai, bre, bim, cmat, dskip.reshape(1, -1), wglu, bglu.reshape(1, -1))


def _flash_t(k_ref, vt_ref, qt_ref, qi, scr, k_block, v_block):
    s0, s1, p0, p1, a0, a1, x0, x1, m_ref, acc_ref = scr
    strips = tuple(range(ATT_LANES // ATT_STRIP))

    def q_off(c):
        return (c * ATT_STRIP) % ATT_TQ

    lower = tuple(c for c in strips if q_off(c) < ATT_TK)
    upper = tuple(c for c in strips if q_off(c) >= ATT_TK)
    m_ref[...] = jnp.full(m_ref.shape, -jnp.inf, F32)
    acc_ref[...] = jnp.zeros(acc_ref.shape, F32)
    p1[...] = jnp.zeros(p1.shape, BF16)
    a1[...] = jnp.ones(a1.shape, F32)
    ones = jnp.ones((ATT_SUM_ROWS, ATT_TK), BF16)

    def scores(j, s_ref, x_ref, which):
        for c in which:
            sl = slice(c * ATT_STRIP, (c + 1) * ATT_STRIP)
            kb = k_ref[0, pl.ds(j * ATT_TK, ATT_TK), k_block(c) * LANES:(k_block(c) + 1) * LANES]
            s = jnp.dot(kb, qt_ref[:, sl], preferred_element_type=F32)
            s_ref[:, sl] = s
            x_ref[:, sl] = jnp.max(s, axis=0, keepdims=True)

    def softmax(s_ref, x_ref, p_ref, a_ref, which, key_off):
        for c in which:
            for t in range(ATT_STRIP // LANES):
                sl = slice(c * ATT_STRIP + t * LANES, c * ATT_STRIP + (t + 1) * LANES)
                s = s_ref[:, sl]
                if key_off is None:
                    s_max = x_ref[:, sl]
                else:
                    kpos = lax.broadcasted_iota(jnp.int32, s.shape, 0) + key_off
                    qpos = lax.broadcasted_iota(jnp.int32, s.shape, 1) + (q_off(c) + t * LANES)
                    s = jnp.where(kpos <= qpos, s, -jnp.inf)
                    s_max = jnp.max(s, axis=0, keepdims=True)
                m_old = m_ref[:, sl]
                m_new = jnp.maximum(m_old, s_max)
                a_ref[:, sl] = jnp.exp2(m_old - m_new)
                m_ref[:, sl] = m_new
                p_ref[:, sl] = jnp.exp2(s - m_new).astype(BF16)

    def value(pair, half, p_ref, a_ref, which):
        for c in which:
            sl = slice(c * ATT_STRIP, (c + 1) * ATT_STRIP)
            vt = vt_ref[0, pair, v_block(c) * LANES:(v_block(c) + 1) * LANES, half * ATT_TK:(half + 1) * ATT_TK]
            pv = jnp.dot(jnp.concatenate([vt, ones], axis=0), p_ref[:, sl], preferred_element_type=F32)
            acc_ref[:, sl] = a_ref[:, sl] * acc_ref[:, sl] + pv

    scores(0, s0, x0, strips)

    def pair_step(i, carry):
        softmax(s0, x0, p0, a0, strips, None)
        scores(2 * i + 1, s1, x1, strips)
        value(jnp.maximum(i - 1, 0), 1, p1, a1, strips)
        softmax(s1, x1, p1, a1, strips, None)
        scores(2 * i + 2, s0, x0, strips)
        value(i, 0, p0, a0, strips)
        return carry

    lax.fori_loop(0, qi, pair_step, 0)
    scores(2 * qi + 1, s1, x1, upper)
    softmax(s0, x0, p0, a0, lower, 0)
    softmax(s0, x0, p0, a0, upper, None)
    value(jnp.maximum(qi - 1, 0), 1, p1, a1, strips)
    softmax(s1, x1, p1, a1, upper, ATT_TK)
    value(qi, 0, p0, a0, strips)
    value(qi, 1, p1, a1, upper)
    return acc_ref[:LANES, :ATT_LANES] / acc_ref[LANES:LANES + 1, :ATT_LANES]


def _mla_attn_body(q_ref, k_ref, vt_ref, o_ref, *scr):
    qi = pl.program_id(2)
    qt_ref = scr[0]
    heads = ATT_LANES // ATT_TQ
    for h in range(heads):
        qt_ref[:, h * ATT_TQ:(h + 1) * ATT_TQ] = q_ref[0, :, h * LANES:(h + 1) * LANES].astype(F32).T.astype(BF16)
    per_head = ATT_TQ // ATT_STRIP
    ot = _flash_t(k_ref, vt_ref, qt_ref, qi, scr[1:], lambda c: c // per_head, lambda c: c // (2 * per_head))
    row = lax.broadcasted_iota(jnp.int32, (LANES, ATT_TQ), 0)
    for r in range(heads // 2):
        lo = ot[:, 2 * r * ATT_TQ:(2 * r + 1) * ATT_TQ]
        hi = ot[:, (2 * r + 1) * ATT_TQ:(2 * r + 2) * ATT_TQ]
        o_ref[0, :, r * LANES:(r + 1) * LANES] = jnp.where(row < MLA_V, lo, hi).T.astype(BF16)


def _attn_specs(s, qk_width, v_width):
    n = ATT_LANES
    row = pltpu.VMEM((1, n), F32)
    return dict(
        in_specs=[pl.BlockSpec((1, ATT_TQ, qk_width), lambda b, h, i: (b, i, h)),
                  pl.BlockSpec((1, s, qk_width), lambda b, h, i: (b, 0, h)),
                  pl.BlockSpec((1, s // ATT_TQ, v_width, ATT_TQ), lambda b, h, i: (b, 0, h, 0))],
        out_specs=pl.BlockSpec((1, ATT_TQ, v_width), lambda b, h, i: (b, i, h)),
        scratch_shapes=[pltpu.VMEM((LANES, n), BF16),
                        pltpu.VMEM((ATT_TK, n + ATT_PAD), F32), pltpu.VMEM((ATT_TK, n + ATT_PAD), F32),
                        pltpu.VMEM((ATT_TK, n), BF16), pltpu.VMEM((ATT_TK, n), BF16),
                        row, row, row, row, row, pltpu.VMEM((LANES + ATT_SUM_ROWS, n + ATT_PAD), F32)],
        compiler_params=_params(("parallel", "parallel", "arbitrary")))


def _mla_attn(q, k, vt):
    nb, s, _ = q.shape
    heads = ATT_LANES // ATT_TQ
    return pl.pallas_call(
        _mla_attn_body,
        grid=(nb, MLA_HEADS // heads, s // ATT_TQ),
        out_shape=jax.ShapeDtypeStruct((nb, s, MLA_HEADS * MLA_V), BF16),
        name="mla_attn", **_attn_specs(s, heads * LANES, heads * MLA_V),
    )(q, k, vt)


def _diff_attn_body(lam_ref, q_ref, k_ref, vt_ref, sub_ref, o_ref, *scr, out_scale):
    qi = pl.program_id(2)
    qt_ref = scr[0]
    heads = ATT_LANES // (2 * ATT_TQ)
    for g in range(heads):
        qt = q_ref[0, :, g * LANES:(g + 1) * LANES].astype(F32).T
        row = lax.broadcasted_iota(jnp.int32, qt.shape, 0)
        zero = jnp.zeros_like(qt)
        qt_ref[:, 2 * g * ATT_TQ:(2 * g + 1) * ATT_TQ] = jnp.where(row < DIFF_HEAD_DIM, qt, zero).astype(BF16)
        qt_ref[:, (2 * g + 1) * ATT_TQ:(2 * g + 2) * ATT_TQ] = jnp.where(row < DIFF_HEAD_DIM, zero, qt).astype(BF16)
    per_head = 2 * ATT_TQ // ATT_STRIP
    ot = _flash_t(k_ref, vt_ref, qt_ref, qi, scr[1:], lambda c: c // per_head, lambda c: c // per_head)
    for g in range(heads):
        o = (ot[:, 2 * g * ATT_TQ:(2 * g + 1) * ATT_TQ]
             - lam_ref[0, 0] * ot[:, (2 * g + 1) * ATT_TQ:(2 * g + 2) * ATT_TQ]).T
        o_ref[0, :, g * LANES:(g + 1) * LANES] = (_rms(o, sub_ref[...]) * out_scale).astype(BF16)


def _diff_attn(lam, q, k, vt, subln, out_scale):
    nb, s, _ = q.shape
    heads = ATT_LANES // (2 * ATT_TQ)
    spec = _attn_specs(s, heads * LANES, heads * LANES)
    spec["in_specs"] = ([pl.BlockSpec(memory_space=pltpu.SMEM)] + spec["in_specs"]
                        + [pl.BlockSpec((1, LANES), lambda b, h, i: (0, 0))])
    return pl.pallas_call(
        functools.partial(_diff_attn_body, out_scale=out_scale),
        grid=(nb, DIFF_HEADS // heads, s // ATT_TQ),
        out_shape=jax.ShapeDtypeStruct((nb, s, DIFF_HEADS * 2 * DIFF_HEAD_DIM), BF16),
        name="diff_attn", **spec,
    )(lam, q, k, vt, subln.reshape(1, -1))


def _rope_table(positions, rot_dim, theta, first_lo, period):
    half = rot_dim // 2
    inv = theta ** (-jnp.arange(half, dtype=F32) * 2.0 / rot_dim)
    ang = inv[:, None] * positions.astype(F32).reshape(1, -1)
    cos, sin = jnp.cos(ang), jnp.sin(ang)
    rel = np.arange(LANES) % period - first_lo
    rotated = ((rel >= 0) & (rel < rot_dim))[:, None]
    freq = np.where(rotated[:, 0], rel % half, 0)
    sign = np.where(rel < half, -1.0, 1.0).astype(np.float32)[:, None]
    cos_t = jnp.where(rotated, cos[freq], 1.0)
    sin_t = jnp.where(rotated, sin[freq] * sign, 0.0)
    return cos_t.T, sin_t.T


def _s5_discretize(lam_re, lam_im, log_dt, b_re, b_im):
    dt = jnp.exp(log_dt)[:, None]
    mag = jnp.exp(lam_re * dt)
    ang = lam_im * dt
    ab_re = mag * jnp.cos(ang)
    ab_im = mag * jnp.sin(ang)
    den = lam_re * lam_re + lam_im * lam_im
    nr = ab_re - 1.0
    f_re = (nr * lam_re + ab_im * lam_im) / den
    f_im = (ab_im * lam_re - nr * lam_im) / den
    bb_re = f_re[..., None] * b_re - f_im[..., None] * b_im
    bb_im = f_re[..., None] * b_im + f_im[..., None] * b_re
    return ab_re, ab_im, bb_re, bb_im


def _block_diag_in(bb):
    eye = jnp.eye(S5_GROUPS, dtype=F32)
    return jnp.einsum('gph,gk->ghkp', bb, eye).reshape(S5_WIDTH, S5_FEATS)


def _block_diag_out(c):
    eye = jnp.eye(S5_GROUPS, dtype=F32)
    return jnp.einsum('ghp,gk->gpkh', c, eye).reshape(S5_FEATS, S5_WIDTH)


def _s5_in_tiles(bb):
    full = _block_diag_in(bb)
    tiles = [full[(n * S5_TILE // S5_EXPAND // S5_TILE) * S5_TILE:, n * S5_TILE:(n + 1) * S5_TILE][:S5_TILE]
             for n in range(S5_FEATS // S5_TILE)]
    return jnp.stack(tiles).astype(BF16)


def _s5_out_tiles(c_re, c_im):
    full_re, full_im = _block_diag_out(c_re), -_block_diag_out(c_im)
    rows = S5_TILE * S5_EXPAND
    tiles = [jnp.concatenate([full_re[k * rows:(k + 1) * rows, k * S5_TILE:(k + 1) * S5_TILE],
                              full_im[k * rows:(k + 1) * rows, k * S5_TILE:(k + 1) * S5_TILE]], axis=0)
             for k in range(S5_WIDTH // S5_TILE)]
    return jnp.stack(tiles).astype(BF16)


def _pad_heads(w, heads, width, lo, hi):
    kdim = w.shape[0]
    w = w.reshape(kdim, heads, width)[:, :, lo:hi]
    w = jnp.pad(w, ((0, 0), (0, 0), (0, LANES - (hi - lo))))
    return w.reshape(kdim, heads * LANES)


def kernel(x, positions, ffn_norm, ffn_w_gu, ffn_w_down, ev_norm, ev_w_in, s5_lambda_re, s5_lambda_im, s5_log_dt, s5_b_re, s5_b_im, s5_c_re, s5_c_im, s5_d, s5_w_glu, s5_b_glu, mla_q_norm, mla_w_uq, mla_kv_norm, mla_w_ukv, ev_w_out, od_norm, od_w_in, diff_lq1, diff_lk1, diff_lq2, diff_lk2, diff_subln, od_w_out, final_norm):
    nb, s, d = x.shape
    t = nb * s
    xt = x.reshape(t, d)
    wgu = ffn_w_gu.astype(BF16)
    wdn = ffn_w_down.astype(BF16)

    xt = _ffn(xt, [], ffn_norm[0, 0], wgu, wdn, 0, 0)

    cos_m, sin_m = _rope_table(positions, MLA_ROPE, MLA_ROPE_THETA, MLA_NOPE, LANES)
    w_in = ev_w_in[0]
    o_kr = S5_WIDTH + MLA_Q_LORA + MLA_KV_LORA
    w_kr = jnp.pad(w_in[:, o_kr:], ((0, 0), (MLA_NOPE, LANES - MLA_NOPE - MLA_ROPE)))
    win_p = jnp.concatenate([w_in[:, :o_kr], w_kr], axis=1).astype(BF16)
    wuq_p = _pad_heads(mla_w_uq[0], MLA_HEADS, MLA_NOPE + MLA_ROPE, 0, MLA_NOPE + MLA_ROPE).astype(BF16)
    wuk_p = _pad_heads(mla_w_ukv[0], MLA_HEADS, MLA_NOPE + MLA_V, 0, MLA_NOPE).astype(BF16)
    wuv = mla_w_ukv[0].reshape(MLA_KV_LORA, MLA_HEADS, MLA_NOPE + MLA_V)[:, :, MLA_NOPE:]
    wuvt = wuv.reshape(MLA_KV_LORA, MLA_HEADS * MLA_V).T.astype(BF16)
    u, q, k, vt = _even_prep(xt, ev_norm[0], win_p, mla_q_norm[0], wuq_p, mla_kv_norm[0], wuk_p, wuvt,
                             cos_m, sin_m, nb)

    ab_re, ab_im, bb_re, bb_im = _s5_discretize(s5_lambda_re[0], s5_lambda_im[0], s5_log_dt[0], s5_b_re[0], s5_b_im[0])
    s5_out = _s5(u.reshape(nb, s, S5_WIDTH), ab_re.reshape(S5_ROWS, LANES), ab_im.reshape(S5_ROWS, LANES),
                 _s5_in_tiles(bb_re), _s5_in_tiles(bb_im), _s5_out_tiles(s5_c_re[0], s5_c_im[0]),
                 s5_d[0].reshape(-1), s5_w_glu[0].astype(BF16), s5_b_glu[0])

    o = _mla_attn(q.reshape(nb, s, -1), k.reshape(nb, s, -1), vt)
    w_out = ev_w_out[0].astype(BF16)
    xt = _ffn(xt, [(s5_out.reshape(t, S5_WIDTH), w_out[:S5_WIDTH]), (o.reshape(t, -1), w_out[S5_WIDTH:])],
              ffn_norm[0, 1], wgu, wdn, 0, 1)

    xt = _ffn(xt, [], ffn_norm[1, 0], wgu, wdn, 1, 0)
    cos_t, sin_t = _rope_table(positions, DIFF_ROT, ROPE_THETA, 0, DIFF_HEAD_DIM)
    width = DIFF_HEADS * 2 * DIFF_HEAD_DIM
    w_in = od_w_in[0]
    q, k, vt = _odd_prep(xt, od_norm[0], w_in[:, :2 * width].astype(BF16), w_in[:, 2 * width:].T.astype(BF16),
                         cos_t, sin_t, nb)
    lam_init = 0.8 - 0.6 * math.exp(-0.3 * 1)
    lam = (jnp.exp(jnp.sum(diff_lq1[0] * diff_lk1[0])) - jnp.exp(jnp.sum(diff_lq2[0] * diff_lk2[0])) + lam_init)
    o = _diff_attn(lam.reshape(1, 1), q.reshape(nb, s, -1), k.reshape(nb, s, -1), vt, diff_subln[0],
                   1.0 - lam_init)
    xt = _ffn(xt, [(o.reshape(t, -1), od_w_out[0].astype(BF16))], ffn_norm[1, 1], wgu, wdn, 1, 1,
              gfinal=final_norm)
    return xt.reshape(nb, s, d)
```

```python
import functools
import math

import jax
import jax.numpy as jnp
import numpy as np
from jax import lax
from jax.experimental import pallas as pl
from jax.experimental.pallas import tpu as pltpu

F32 = jnp.float32
BF16 = jnp.bfloat16

D_MODEL = 1024
NORM_EPS = 1e-6
D_FF = 2816
S5_WIDTH = 512
S5_GROUP_CH = 16
S5_GROUPS = 32
S5_STATE = 64
S5_FEATS = S5_GROUPS * S5_STATE
MLA_HEADS = 8
MLA_NOPE = 64
MLA_ROPE = 32
MLA_V = 64
MLA_Q_LORA = 384
MLA_KV_LORA = 256
MLA_ROPE_THETA = 10000.0
DIFF_HEADS = 8
DIFF_HEAD_DIM = 64
DIFF_ROT = 16
ROPE_THETA = 500000.0

LANES = 128
SUBLANES = 8
VMEM_LIMIT = 56 * 1024 * 1024
LOG2E = math.log2(math.e)

FFN_TM = 512
FFN_CHUNKS = 1
S5_TS = 256
S5_PITCH = S5_TS + 4
S5_ROWS = S5_FEATS // LANES
S5_TILE = 256
S5_EXPAND = S5_FEATS // S5_WIDTH
ATT_TQ = 1024
ATT_TK = ATT_TQ // 2
ATT_LANES = 2 * ATT_TQ
ATT_STRIP = 256
ATT_SUM_ROWS = 16
ATT_PAD = LANES
PREP_TM = ATT_TQ

_NT = (((1,), (1,)), ((), ()))


def _rms(x, g):
    ms = jnp.mean(x * x, axis=-1, keepdims=True)
    return x * lax.rsqrt(ms + NORM_EPS) * g


def _const_spec(shape):
    nd = len(shape)
    return pl.BlockSpec(shape, lambda *_: (0,) * nd, pipeline_mode=pl.Buffered(1))


def _params(sem, flags=None):
    return pltpu.CompilerParams(dimension_semantics=sem, vmem_limit_bytes=VMEM_LIMIT, flags=flags)


def _ffn_body(*refs, n_mix, final):
    x_ref = refs[0]
    mix_refs = refs[1:1 + 2 * n_mix]
    g_ref, wgu_ref, wd_ref = refs[1 + 2 * n_mix:4 + 2 * n_mix]
    gf_ref = refs[4 + 2 * n_mix] if final else None
    o_ref = refs[-1]

    x = x_ref[...]
    for i in range(n_mix):
        x = x + jnp.dot(mix_refs[2 * i][...], mix_refs[2 * i + 1][...], preferred_element_type=F32)
    h = _rms(x, g_ref[...]).astype(BF16)
    ck = D_FF // FFN_CHUNKS
    y = None
    for c in range(FFN_CHUNKS):
        g = jnp.dot(h, wgu_ref[:, c * ck:(c + 1) * ck], preferred_element_type=F32)
        u = jnp.dot(h, wgu_ref[:, D_FF + c * ck:D_FF + (c + 1) * ck], preferred_element_type=F32)
        a = (g * jax.nn.sigmoid(g) * u).astype(BF16)
        d = jnp.dot(a, wd_ref[c * ck:(c + 1) * ck, :], preferred_element_type=F32)
        y = d if y is None else y + d
    out = x + 0.5 * y
    if final:
        out = _rms(out, gf_ref[...])
    o_ref[...] = out


def _ffn(x, mixes, g, wgu, wd, layer, pos, gfinal=None):
    t = x.shape[0]
    n_mix = len(mixes)
    final = gfinal is not None
    in_specs = [pl.BlockSpec((FFN_TM, D_MODEL), lambda i: (i, 0))]
    args = [x]
    for m, w in mixes:
        in_specs += [pl.BlockSpec((FFN_TM, m.shape[1]), lambda i: (i, 0)), _const_spec(w.shape)]
        args += [m, w]

    def slab(w):
        return pl.BlockSpec((None, None) + w.shape[2:], lambda i: (layer, pos, 0, 0), pipeline_mode=pl.Buffered(1))

    in_specs += [_const_spec((1, D_MODEL)), slab(wgu), slab(wd)]
    args += [g.reshape(1, D_MODEL), wgu, wd]
    if final:
        in_specs.append(_const_spec((1, D_MODEL)))
        args.append(gfinal.reshape(1, D_MODEL))
    return pl.pallas_call(
        functools.partial(_ffn_body, n_mix=n_mix, final=final),
        grid=(t // FFN_TM,),
        in_specs=in_specs,
        out_specs=pl.BlockSpec((FFN_TM, D_MODEL), lambda i: (i, 0)),
        out_shape=jax.ShapeDtypeStruct((t, D_MODEL), F32),
        compiler_params=_params(("parallel",)),
        name="ffn",
    )(*args)


def _rope(blk, cos, sin, half, first_lo, period):
    lane = lax.broadcasted_iota(jnp.int32, blk.shape, 1) % period
    fwd = pltpu.roll(blk, LANES - half, 1)
    bwd = pltpu.roll(blk, half, 1)
    rot = jnp.where(lane < first_lo + half, fwd, bwd)
    return blk * cos + rot * sin


def _even_prep_body(x_ref, g_ref, win_ref, qn_ref, wuq_ref, kvn_ref, wuk_ref, wuvt_ref, cos_ref, sin_ref,
                    u_ref, q_ref, k_ref, vt_ref):
    h = _rms(x_ref[...], g_ref[...]).astype(BF16)
    proj = jnp.dot(h, win_ref[...], preferred_element_type=F32)
    o_cq = S5_WIDTH
    o_ckv = o_cq + MLA_Q_LORA
    o_kr = o_ckv + MLA_KV_LORA
    u_ref[...] = proj[:, :o_cq]
    cq = proj[:, o_cq:o_ckv]
    ckv = proj[:, o_ckv:o_kr]
    kr = proj[:, o_kr:o_kr + LANES]
    cos = cos_ref[...]
    sin = sin_ref[...]
    half = MLA_ROPE // 2
    scale = (MLA_NOPE + MLA_ROPE) ** -0.5 * LOG2E
    q = jnp.dot(_rms(cq, qn_ref[...]).astype(BF16), wuq_ref[...], preferred_element_type=F32)
    ckn = _rms(ckv, kvn_ref[...]).astype(BF16)
    kn = jnp.dot(ckn, wuk_ref[...], preferred_element_type=F32)
    vt_ref[0, 0] = lax.dot_general(wuvt_ref[...], ckn, _NT, preferred_element_type=F32).astype(BF16)
    kpe = _rope(kr, cos, sin, half, MLA_NOPE, LANES)
    for hd in range(MLA_HEADS):
        sl = slice(hd * LANES, (hd + 1) * LANES)
        q_ref[:, sl] = (_rope(q[:, sl], cos, sin, half, MLA_NOPE, LANES) * scale).astype(BF16)
        k_ref[:, sl] = (kn[:, sl] + kpe).astype(BF16)


def _vt_spec(nblk, rows):
    return pl.BlockSpec((1, 1, rows, PREP_TM), lambda i: (i // nblk, i % nblk, 0, 0))


def _even_prep(x, g, win, qn, wuq, kvn, wuk, wuvt, cos, sin, nb):
    t = x.shape[0]
    tm = PREP_TM
    nblk = t // nb // tm
    row = lambda w: pl.BlockSpec((tm, w), lambda i: (i, 0))
    return pl.pallas_call(
        _even_prep_body,
        grid=(t // tm,),
        in_specs=[row(D_MODEL), _const_spec((1, D_MODEL)), _const_spec(win.shape),
                  _const_spec((1, MLA_Q_LORA)), _const_spec(wuq.shape),
                  _const_spec((1, MLA_KV_LORA)), _const_spec(wuk.shape), _const_spec(wuvt.shape),
                  row(LANES), row(LANES)],
        out_specs=[row(S5_WIDTH), row(MLA_HEADS * LANES), row(MLA_HEADS * LANES),
                   _vt_spec(nblk, MLA_HEADS * MLA_V)],
        out_shape=[jax.ShapeDtypeStruct((t, S5_WIDTH), F32),
                   jax.ShapeDtypeStruct((t, MLA_HEADS * LANES), BF16),
                   jax.ShapeDtypeStruct((t, MLA_HEADS * LANES), BF16),
                   jax.ShapeDtypeStruct((nb, nblk, MLA_HEADS * MLA_V, tm), BF16)],
        compiler_params=_params(("parallel",)),
        name="even_prep",
    )(x, g.reshape(1, -1), win, qn.reshape(1, -1), wuq, kvn.reshape(1, -1), wuk, wuvt, cos, sin)


def _odd_prep_body(x_ref, g_ref, wqk_ref, wvt_ref, cos_ref, sin_ref, q_ref, k_ref, vt_ref):
    h = _rms(x_ref[...], g_ref[...]).astype(BF16)
    cos = cos_ref[...]
    sin = sin_ref[...]
    half = DIFF_ROT // 2
    width = DIFF_HEADS * 2 * DIFF_HEAD_DIM
    scale = DIFF_HEAD_DIM ** -0.5 * LOG2E
    q = jnp.dot(h, wqk_ref[:, :width], preferred_element_type=F32)
    k = jnp.dot(h, wqk_ref[:, width:], preferred_element_type=F32)
    vt_ref[0, 0] = lax.dot_general(wvt_ref[...], h, _NT, preferred_element_type=F32).astype(BF16)
    for c in range(width // LANES):
        sl = slice(c * LANES, (c + 1) * LANES)
        q_ref[:, sl] = (_rope(q[:, sl], cos, sin, half, 0, DIFF_HEAD_DIM) * scale).astype(BF16)
        k_ref[:, sl] = _rope(k[:, sl], cos, sin, half, 0, DIFF_HEAD_DIM).astype(BF16)


def _odd_prep(x, g, wqk, wvt, cos, sin, nb):
    t = x.shape[0]
    tm = PREP_TM
    nblk = t // nb // tm
    width = DIFF_HEADS * 2 * DIFF_HEAD_DIM
    row = lambda w: pl.BlockSpec((tm, w), lambda i: (i, 0))
    return pl.pallas_call(
        _odd_prep_body,
        grid=(t // tm,),
        in_specs=[row(D_MODEL), _const_spec((1, D_MODEL)), _const_spec(wqk.shape), _const_spec(wvt.shape),
                  row(LANES), row(LANES)],
        out_specs=[row(width), row(width), _vt_spec(nblk, width)],
        out_shape=[jax.ShapeDtypeStruct((t, width), BF16), jax.ShapeDtypeStruct((t, width), BF16),
                   jax.ShapeDtypeStruct((nb, nblk, width, tm), BF16)],
        compiler_params=_params(("parallel",)),
        name="odd_prep",
    )(x, g.reshape(1, -1), wqk, wvt, cos, sin)


def _s5_body(u_ref, ar_ref, ai_ref, bre_ref, bim_ref, c_ref, d_ref, wglu_ref, bglu_ref, o_ref,
             xr_ref, xi_ref, sr_ref, si_ref):
    nb = u_ref.shape[0]

    @pl.when(pl.program_id(0) == 0)
    def _():
        sr_ref[...] = jnp.zeros_like(sr_ref)
        si_ref[...] = jnp.zeros_like(si_ref)

    for b in range(nb):
        ub = u_ref[b].astype(BF16)
        for n in range(S5_FEATS // S5_TILE):
            k = n * S5_TILE // S5_EXPAND // S5_TILE
            uk = ub[:, k * S5_TILE:(k + 1) * S5_TILE]
            bur = jnp.dot(uk, bre_ref[n], preferred_element_type=F32)
            bui = jnp.dot(uk, bim_ref[n], preferred_element_type=F32)
            for h in range(S5_TILE // LANES):
                j = n * (S5_TILE // LANES) + h
                xr_ref[b, j * S5_PITCH:j * S5_PITCH + S5_TS, :] = bur[:, h * LANES:(h + 1) * LANES]
                xi_ref[b, j * S5_PITCH:j * S5_PITCH + S5_TS, :] = bui[:, h * LANES:(h + 1) * LANES]

    ar = ar_ref[...]
    ai = ai_ref[...]

    def step(t, carry):
        new = []
        for b in range(nb):
            pr, pi = carry[2 * b], carry[2 * b + 1]
            idx = pl.ds(t, S5_ROWS, stride=S5_PITCH)
            nr = ar * pr - ai * pi + xr_ref[b, idx, :]
            ni = ar * pi + ai * pr + xi_ref[b, idx, :]
            xr_ref[b, idx, :] = nr
            xi_ref[b, idx, :] = ni
            new += [nr, ni]
        return tuple(new)

    init = []
    for b in range(nb):
        init += [sr_ref[b], si_ref[b]]
    fin = lax.fori_loop(0, S5_TS, step, tuple(init), unroll=4)
    for b in range(nb):
        sr_ref[b] = fin[2 * b]
        si_ref[b] = fin[2 * b + 1]

    per_tile = S5_ROWS * S5_TILE // S5_WIDTH
    for b in range(nb):
        ys = []
        for k in range(S5_WIDTH // S5_TILE):
            slabs = range(k * per_tile, (k + 1) * per_tile)
            parts = [xr_ref[b, j * S5_PITCH:j * S5_PITCH + S5_TS, :].astype(BF16) for j in slabs]
            parts += [xi_ref[b, j * S5_PITCH:j * S5_PITCH + S5_TS, :].astype(BF16) for j in slabs]
            ys.append(jnp.dot(jnp.concatenate(parts, axis=1), c_ref[k], preferred_element_type=F32))
        y = jnp.concatenate(ys, axis=1) + d_ref[...] * u_ref[b]
        g = 0.5 * y * (1.0 + jnp.tanh(math.sqrt(2.0 / math.pi) * (y + 0.044715 * (y * y * y))))
        z = jnp.dot(g.astype(BF16), wglu_ref[...], preferred_element_type=F32) + bglu_ref[...]
        o_ref[b] = (g * jax.nn.sigmoid(z)).astype(BF16)


def _s5(u, ar, ai, bre, bim, cmat, dskip, wglu, bglu):
    nb, s, _ = u.shape
    blk = pl.BlockSpec((nb, S5_TS, S5_WIDTH), lambda i: (0, i, 0))
    return pl.pallas_call(
        _s5_body,
        grid=(s // S5_TS,),
        in_specs=[blk, _const_spec(ar.shape), _const_spec(ai.shape), _const_spec(bre.shape),
                  _const_spec(bim.shape), _const_spec(cmat.shape), _const_spec((1, S5_WIDTH)),
                  _const_spec(wglu.shape), _const_spec((1, S5_WIDTH))],
        out_specs=blk,
        out_shape=jax.ShapeDtypeStruct((nb, s, S5_WIDTH), BF16),
        scratch_shapes=[pltpu.VMEM((nb, S5_ROWS * S5_PITCH, LANES), F32),
                        pltpu.VMEM((nb, S5_ROWS * S5_PITCH, LANES), F32),
                        pltpu.VMEM((nb, S5_ROWS, LANES), F32),
                        pltpu.VMEM((nb, S5_ROWS, LANES), F32)],
        compiler_params=_params(("arbitrary",)),
        name="s5",
    )(u, ar, ai, bre, bim, cmat, dskip.reshape(1, -1), wglu, bglu.reshape(1, -1))


def _flash_t(k_ref, vt_ref, qt_ref, qi, scr, k_block, v_block):
    s0, s1, p0, p1, a0, a1, x0, x1, m_ref, acc_ref = scr
    strips = tuple(range(ATT_LANES // ATT_STRIP))

    def q_off(c):
        return (c * ATT_STRIP) % ATT_TQ

    lower = tuple(c for c in strips if q_off(c) < ATT_TK)
    upper = tuple(c for c in strips if q_off(c) >= ATT_TK)
    m_ref[...] = jnp.full(m_ref.shape, -jnp.inf, F32)
    acc_ref[...] = jnp.zeros(acc_ref.shape, F32)
    p1[...] = jnp.zeros(p1.shape, BF16)
    a1[...] = jnp.ones(a1.shape, F32)
    ones = jnp.ones((ATT_SUM_ROWS, ATT_TK), BF16)

    def scores(j, s_ref, x_ref, which):
        for c in which:
            sl = slice(c * ATT_STRIP, (c + 1) * ATT_STRIP)
            kb = k_ref[0, pl.ds(j * ATT_TK, ATT_TK), k_block(c) * LANES:(k_block(c) + 1) * LANES]
            s = jnp.dot(kb, qt_ref[:, sl], preferred_element_type=F32)
            s_ref[:, sl] = s
            x_ref[:, sl] = jnp.max(s, axis=0, keepdims=True)

    def softmax(s_ref, x_ref, p_ref, a_ref, which, key_off):
        for c in which:
            for t in range(ATT_STRIP // LANES):
                sl = slice(c * ATT_STRIP + t * LANES, c * ATT_STRIP + (t + 1) * LANES)
                s = s_ref[:, sl]
                if key_off is None:
                    s_max = x_ref[:, sl]
                else:
                    kpos = lax.broadcasted_iota(jnp.int32, s.shape, 0) + key_off
                    qpos = lax.broadcasted_iota(jnp.int32, s.shape, 1) + (q_off(c) + t * LANES)
                    s = jnp.where(kpos <= qpos, s, -jnp.inf)
                    s_max = jnp.max(s, axis=0, keepdims=True)
                m_old = m_ref[:, sl]
                m_new = jnp.maximum(m_old, s_max)
                a_ref[:, sl] = jnp.exp2(m_old - m_new)
                m_ref[:, sl] = m_new
                p_ref[:, sl] = jnp.exp2(s - m_new).astype(BF16)

    def value(pair, half, p_ref, a_ref, which):
        for c in which:
            sl = slice(c * ATT_STRIP, (c + 1) * ATT_STRIP)
            vt = vt_ref[0, pair, v_block(c) * LANES:(v_block(c) + 1) * LANES, half * ATT_TK:(half + 1) * ATT_TK]
            pv = jnp.dot(jnp.concatenate([vt, ones], axis=0), p_ref[:, sl], preferred_element_type=F32)
            acc_ref[:, sl] = a_ref[:, sl] * acc_ref[:, sl] + pv

    scores(0, s0, x0, strips)

    def pair_step(i, carry):
        softmax(s0, x0, p0, a0, strips, None)
        scores(2 * i + 1, s1, x1, strips)
        value(jnp.maximum(i - 1, 0), 1, p1, a1, strips)
        softmax(s1, x1, p1, a1, strips, None)
        scores(2 * i + 2, s0, x0, strips)
        value(i, 0, p0, a0, strips)
        return carry

    lax.fori_loop(0, qi, pair_step, 0)
    scores(2 * qi + 1, s1, x1, upper)
    softmax(s0, x0, p0, a0, lower, 0)
    softmax(s0, x0, p0, a0, upper, None)
    value(jnp.maximum(qi - 1, 0), 1, p1, a1, strips)
    softmax(s1, x1, p1, a1, upper, ATT_TK)
    value(qi, 0, p0, a0, strips)
    value(qi, 1, p1, a1, upper)
    return acc_ref[:LANES, :ATT_LANES] / acc_ref[LANES:LANES + 1, :ATT_LANES]


def _mla_attn_body(q_ref, k_ref, vt_ref, o_ref, *scr):
    qi = pl.program_id(2)
    qt_ref = scr[0]
    heads = ATT_LANES // ATT_TQ
    for h in range(heads):
        qt_ref[:, h * ATT_TQ:(h + 1) * ATT_TQ] = q_ref[0, :, h * LANES:(h + 1) * LANES].astype(F32).T.astype(BF16)
    per_head = ATT_TQ // ATT_STRIP
    ot = _flash_t(k_ref, vt_ref, qt_ref, qi, scr[1:], lambda c: c // per_head, lambda c: c // (2 * per_head))
    row = lax.broadcasted_iota(jnp.int32, (LANES, ATT_TQ), 0)
    for r in range(heads // 2):
        lo = ot[:, 2 * r * ATT_TQ:(2 * r + 1) * ATT_TQ]
        hi = ot[:, (2 * r + 1) * ATT_TQ:(2 * r + 2) * ATT_TQ]
        o_ref[0, :, r * LANES:(r + 1) * LANES] = jnp.where(row < MLA_V, lo, hi).T.astype(BF16)


def _attn_specs(s, qk_width, v_width):
    n = ATT_LANES
    row = pltpu.VMEM((1, n), F32)
    return dict(
        in_specs=[pl.BlockSpec((1, ATT_TQ, qk_width), lambda b, h, i: (b, i, h)),
                  pl.BlockSpec((1, s, qk_width), lambda b, h, i: (b, 0, h)),
                  pl.BlockSpec((1, s // ATT_TQ, v_width, ATT_TQ), lambda b, h, i: (b, 0, h, 0))],
        out_specs=pl.BlockSpec((1, ATT_TQ, v_width), lambda b, h, i: (b, i, h)),
        scratch_shapes=[pltpu.VMEM((LANES, n), BF16),
                        pltpu.VMEM((ATT_TK, n + ATT_PAD), F32), pltpu.VMEM((ATT_TK, n + ATT_PAD), F32),
                        pltpu.VMEM((ATT_TK, n), BF16), pltpu.VMEM((ATT_TK, n), BF16),
                        row, row, row, row, row, pltpu.VMEM((LANES + ATT_SUM_ROWS, n + ATT_PAD), F32)],
        compiler_params=_params(("parallel", "parallel", "arbitrary")))


def _mla_attn(q, k, vt):
    nb, s, _ = q.shape
    heads = ATT_LANES // ATT_TQ
    return pl.pallas_call(
        _mla_attn_body,
        grid=(nb, MLA_HEADS // heads, s // ATT_TQ),
        out_shape=jax.ShapeDtypeStruct((nb, s, MLA_HEADS * MLA_V), BF16),
        name="mla_attn", **_attn_specs(s, heads * LANES, heads * MLA_V),
    )(q, k, vt)


def _diff_attn_body(lam_ref, q_ref, k_ref, vt_ref, sub_ref, o_ref, *scr, out_scale):
    qi = pl.program_id(2)
    qt_ref = scr[0]
    heads = ATT_LANES // (2 * ATT_TQ)
    for g in range(heads):
        qt = q_ref[0, :, g * LANES:(g + 1) * LANES].astype(F32).T
        row = lax.broadcasted_iota(jnp.int32, qt.shape, 0)
        zero = jnp.zeros_like(qt)
        qt_ref[:, 2 * g * ATT_TQ:(2 * g + 1) * ATT_TQ] = jnp.where(row < DIFF_HEAD_DIM, qt, zero).astype(BF16)
        qt_ref[:, (2 * g + 1) * ATT_TQ:(2 * g + 2) * ATT_TQ] = jnp.where(row < DIFF_HEAD_DIM, zero, qt).astype(BF16)
    per_head = 2 * ATT_TQ // ATT_STRIP
    ot = _flash_t(k_ref, vt_ref, qt_ref, qi, scr[1:], lambda c: c // per_head, lambda c: c // per_head)
    for g in range(heads):
        o = (ot[:, 2 * g * ATT_TQ:(2 * g + 1) * ATT_TQ]
             - lam_ref[0, 0] * ot[:, (2 * g + 1) * ATT_TQ:(2 * g + 2) * ATT_TQ]).T
        o_ref[0, :, g * LANES:(g + 1) * LANES] = (_rms(o, sub_ref[...]) * out_scale).astype(BF16)


def _diff_attn(lam, q, k, vt, subln, out_scale):
    nb, s, _ = q.shape
    heads = ATT_LANES // (2 * ATT_TQ)
    spec = _attn_specs(s, heads * LANES, heads * LANES)
    spec["in_specs"] = ([pl.BlockSpec(memory_space=pltpu.SMEM)] + spec["in_specs"]
                        + [pl.BlockSpec((1, LANES), lambda b, h, i: (0, 0))])
    return pl.pallas_call(
        functools.partial(_diff_attn_body, out_scale=out_scale),
        grid=(nb, DIFF_HEADS // heads, s // ATT_TQ),
        out_shape=jax.ShapeDtypeStruct((nb, s, DIFF_HEADS * 2 * DIFF_HEAD_DIM), BF16),
        name="diff_attn", **spec,
    )(lam, q, k, vt, subln.reshape(1, -1))


def _rope_table(positions, rot_dim, theta, first_lo, period):
    half = rot_dim // 2
    inv = theta ** (-jnp.arange(half, dtype=F32) * 2.0 / rot_dim)
    ang = inv[:, None] * positions.astype(F32).reshape(1, -1)
    cos, sin = jnp.cos(ang), jnp.sin(ang)
    rel = np.arange(LANES) % period - first_lo
    rotated = ((rel >= 0) & (rel < rot_dim))[:, None]
    freq = np.where(rotated[:, 0], rel % half, 0)
    sign = np.where(rel < half, -1.0, 1.0).astype(np.float32)[:, None]
    cos_t = jnp.where(rotated, cos[freq], 1.0)
    sin_t = jnp.where(rotated, sin[freq] * sign, 0.0)
    return cos_t.T, sin_t.T


def _s5_discretize(lam_re, lam_im, log_dt, b_re, b_im):
    dt = jnp.exp(log_dt)[:, None]
    mag = jnp.exp(lam_re * dt)
    ang = lam_im * dt
    ab_re = mag * jnp.cos(ang)
    ab_im = mag * jnp.sin(ang)
    den = lam_re * lam_re + lam_im * lam_im
    nr = ab_re - 1.0
    f_re = (nr * lam_re + ab_im * lam_im) / den
    f_im = (ab_im * lam_re - nr * lam_im) / den
    bb_re = f_re[..., None] * b_re - f_im[..., None] * b_im
    bb_im = f_re[..., None] * b_im + f_im[..., None] * b_re
    return ab_re, ab_im, bb_re, bb_im


def _block_diag_in(bb):
    eye = jnp.eye(S5_GROUPS, dtype=F32)
    return jnp.einsum('gph,gk->ghkp', bb, eye).reshape(S5_WIDTH, S5_FEATS)


def _block_diag_out(c):
    eye = jnp.eye(S5_GROUPS, dtype=F32)
    return jnp.einsum('ghp,gk->gpkh', c, eye).reshape(S5_FEATS, S5_WIDTH)


def _s5_in_tiles(bb):
    full = _block_diag_in(bb)
    tiles = [full[(n * S5_TILE // S5_EXPAND // S5_TILE) * S5_TILE:, n * S5_TILE:(n + 1) * S5_TILE][:S5_TILE]
             for n in range(S5_FEATS // S5_TILE)]
    return jnp.stack(tiles).astype(BF16)


def _s5_out_tiles(c_re, c_im):
    full_re, full_im = _block_diag_out(c_re), -_block_diag_out(c_im)
    rows = S5_TILE * S5_EXPAND
    tiles = [jnp.concatenate([full_re[k * rows:(k + 1) * rows, k * S5_TILE:(k + 1) * S5_TILE],
                              full_im[k * rows:(k + 1) * rows, k * S5_TILE:(k + 1) * S5_TILE]], axis=0)
             for k in range(S5_WIDTH // S5_TILE)]
    return jnp.stack(tiles).astype(BF16)


def _pad_heads(w, heads, width, lo, hi):
    kdim = w.shape[0]
    w = w.reshape(kdim, heads, width)[:, :, lo:hi]
    w = jnp.pad(w, ((0, 0), (0, 0), (0, LANES - (hi - lo))))
    return w.reshape(kdim, heads * LANES)


def kernel(x, positions, ffn_norm, ffn_w_gu, ffn_w_down, ev_norm, ev_w_in, s5_lambda_re, s5_lambda_im, s5_log_dt, s5_b_re, s5_b_im, s5_c_re, s5_c_im, s5_d, s5_w_glu, s5_b_glu, mla_q_norm, mla_w_uq, mla_kv_norm, mla_w_ukv, ev_w_out, od_norm, od_w_in, diff_lq1, diff_lk1, diff_lq2, diff_lk2, diff_subln, od_w_out, final_norm):
    nb, s, d = x.shape
    t = nb * s
    xt = x.reshape(t, d)
    wgu = ffn_w_gu.astype(BF16)
    wdn = ffn_w_down.astype(BF16)

    xt = _ffn(xt, [], ffn_norm[0, 0], wgu, wdn, 0, 0)

    cos_m, sin_m = _rope_table(positions, MLA_ROPE, MLA_ROPE_THETA, MLA_NOPE, LANES)
    w_in = ev_w_in[0]
    o_kr = S5_WIDTH + MLA_Q_LORA + MLA_KV_LORA
    w_kr = jnp.pad(w_in[:, o_kr:], ((0, 0), (MLA_NOPE, LANES - MLA_NOPE - MLA_ROPE)))
    win_p = jnp.concatenate([w_in[:, :o_kr], w_kr], axis=1).astype(BF16)
    wuq_p = _pad_heads(mla_w_uq[0], MLA_HEADS, MLA_NOPE + MLA_ROPE, 0, MLA_NOPE + MLA_ROPE).astype(BF16)
    wuk_p = _pad_heads(mla_w_ukv[0], MLA_HEADS, MLA_NOPE + MLA_V, 0, MLA_NOPE).astype(BF16)
    wuv = mla_w_ukv[0].reshape(MLA_KV_LORA, MLA_HEADS, MLA_NOPE + MLA_V)[:, :, MLA_NOPE:]
    wuvt = wuv.reshape(MLA_KV_LORA, MLA_HEADS * MLA_V).T.astype(BF16)
    u, q, k, vt = _even_prep(xt, ev_norm[0], win_p, mla_q_norm[0], wuq_p, mla_kv_norm[0], wuk_p, wuvt,
                             cos_m, sin_m, nb)

    ab_re, ab_im, bb_re, bb_im = _s5_discretize(s5_lambda_re[0], s5_lambda_im[0], s5_log_dt[0], s5_b_re[0], s5_b_im[0])
    s5_out = _s5(u.reshape(nb, s, S5_WIDTH), ab_re.reshape(S5_ROWS, LANES), ab_im.reshape(S5_ROWS, LANES),
                 _s5_in_tiles(bb_re), _s5_in_tiles(bb_im), _s5_out_tiles(s5_c_re[0], s5_c_im[0]),
                 s5_d[0].reshape(-1), s5_w_glu[0].astype(BF16), s5_b_glu[0])

    o = _mla_attn(q.reshape(nb, s, -1), k.reshape(nb, s, -1), vt)
    w_out = ev_w_out[0].astype(BF16)
    xt = _ffn(xt, [(s5_out.reshape(t, S5_WIDTH), w_out[:S5_WIDTH]), (o.reshape(t, -1), w_out[S5_WIDTH:])],
              ffn_norm[0, 1], wgu, wdn, 0, 1)

    xt = _ffn(xt, [], ffn_norm[1, 0], wgu, wdn, 1, 0)
    cos_t, sin_t = _rope_table(positions, DIFF_ROT, ROPE_THETA, 0, DIFF_HEAD_DIM)
    width = DIFF_HEADS * 2 * DIFF_HEAD_DIM
    w_in = od_w_in[0]
    q, k, vt = _odd_prep(xt, od_norm[0], w_in[:, :2 * width].astype(BF16), w_in[:, 2 * width:].T.astype(BF16),
                         cos_t, sin_t, nb)
    lam_init = 0.8 - 0.6 * math.exp(-0.3 * 1)
    lam = (jnp.exp(jnp.sum(diff_lq1[0] * diff_lk1[0])) - jnp.exp(jnp.sum(diff_lq2[0] * diff_lk2[0])) + lam_init)
    o = _diff_attn(lam.reshape(1, 1), q.reshape(nb, s, -1), k.reshape(nb, s, -1), vt, diff_subln[0],
                   1.0 - lam_init)
    xt = _ffn(xt, [(o.reshape(t, -1), od_w_out[0].astype(BF16))], ffn_norm[1, 1], wgu, wdn, 1, 1,
              gfinal=final_norm)
    return xt.reshape(nb, s, d)
```

```python
import functools
import math

import jax
import jax.numpy as jnp
import numpy as np
from jax import lax
from jax.experimental import pallas as pl
from jax.experimental.pallas import tpu as pltpu

F32 = jnp.float32
BF16 = jnp.bfloat16

D_MODEL = 1024
NORM_EPS = 1e-6
D_FF = 2816
S5_WIDTH = 512
S5_GROUP_CH = 16
S5_GROUPS = 32
S5_STATE = 64
S5_FEATS = S5_GROUPS * S5_STATE
MLA_HEADS = 8
MLA_NOPE = 64
MLA_ROPE = 32
MLA_V = 64
MLA_Q_LORA = 384
MLA_KV_LORA = 256
MLA_ROPE_THETA = 10000.0
DIFF_HEADS = 8
DIFF_HEAD_DIM = 64
DIFF_ROT = 16
ROPE_THETA = 500000.0

LANES = 128
SUBLANES = 8
VMEM_LIMIT = 56 * 1024 * 1024
LOG2E = math.log2(math.e)

FFN_TM = 512
FFN_CHUNKS = 1
S5_TS = 256
S5_PITCH = S5_TS + 4
S5_ROWS = S5_FEATS // LANES
S5_TILE = 256
S5_EXPAND = S5_FEATS // S5_WIDTH
ATT_TQ = 1024
ATT_TK = ATT_TQ // 2
ATT_LANES = 2 * ATT_TQ
ATT_STRIP = 256
ATT_SUM_ROWS = 16
ATT_PAD = LANES
PREP_TM = ATT_TQ

_NT = (((1,), (1,)), ((), ()))


def _rms(x, g):
    ms = jnp.mean(x * x, axis=-1, keepdims=True)
    return x * lax.rsqrt(ms + NORM_EPS) * g


def _const_spec(shape):
    nd = len(shape)
    return pl.BlockSpec(shape, lambda *_: (0,) * nd, pipeline_mode=pl.Buffered(1))


def _params(sem, flags=None):
    return pltpu.CompilerParams(dimension_semantics=sem, vmem_limit_bytes=VMEM_LIMIT, flags=flags)


def _ffn_body(*refs, n_mix, final):
    x_ref = refs[0]
    mix_refs = refs[1:1 + 2 * n_mix]
    g_ref, wgu_ref, wd_ref = refs[1 + 2 * n_mix:4 + 2 * n_mix]
    gf_ref = refs[4 + 2 * n_mix] if final else None
    o_ref = refs[-1]

    x = x_ref[...]
    for i in range(n_mix):
        x = x + jnp.dot(mix_refs[2 * i][...], mix_refs[2 * i + 1][...], preferred_element_type=F32)
    h = _rms(x, g_ref[...]).astype(BF16)
    ck = D_FF // FFN_CHUNKS
    y = None
    for c in range(FFN_CHUNKS):
        g = jnp.dot(h, wgu_ref[:, c * ck:(c + 1) * ck], preferred_element_type=F32)
        u = jnp.dot(h, wgu_ref[:, D_FF + c * ck:D_FF + (c + 1) * ck], preferred_element_type=F32)
        a = (g * jax.nn.sigmoid(g) * u).astype(BF16)
        d = jnp.dot(a, wd_ref[c * ck:(c + 1) * ck, :], preferred_element_type=F32)
        y = d if y is None else y + d
    out = x + 0.5 * y
    if final:
        out = _rms(out, gf_ref[...])
    o_ref[...] = out


def _ffn(x, mixes, g, wgu, wd, layer, pos, gfinal=None):
    t = x.shape[0]
    n_mix = len(mixes)
    final = gfinal is not None
    in_specs = [pl.BlockSpec((FFN_TM, D_MODEL), lambda i: (i, 0))]
    args = [x]
    for m, w in mixes:
        in_specs += [pl.BlockSpec((FFN_TM, m.shape[1]), lambda i: (i, 0)), _const_spec(w.shape)]
        args += [m, w]

    def slab(w):
        return pl.BlockSpec((None, None) + w.shape[2:], lambda i: (layer, pos, 0, 0), pipeline_mode=pl.Buffered(1))

    in_specs += [_const_spec((1, D_MODEL)), slab(wgu), slab(wd)]
    args += [g.reshape(1, D_MODEL), wgu, wd]
    if final:
        in_specs.append(_const_spec((1, D_MODEL)))
        args.append(gfinal.reshape(1, D_MODEL))
    return pl.pallas_call(
        functools.partial(_ffn_body, n_mix=n_mix, final=final),
        grid=(t // FFN_TM,),
        in_specs=in_specs,
        out_specs=pl.BlockSpec((FFN_TM, D_MODEL), lambda i: (i, 0)),
        out_shape=jax.ShapeDtypeStruct((t, D_MODEL), F32),
        compiler_params=_params(("parallel",)),
        name="ffn",
    )(*args)


def _rope(blk, cos, sin, half, first_lo, period):
    lane = lax.broadcasted_iota(jnp.int32, blk.shape, 1) % period
    fwd = pltpu.roll(blk, LANES - half, 1)
    bwd = pltpu.roll(blk, half, 1)
    rot = jnp.where(lane < first_lo + half, fwd, bwd)
    return blk * cos + rot * sin


def _even_prep_body(x_ref, g_ref, win_ref, qn_ref, wuq_ref, kvn_ref, wuk_ref, wuvt_ref, cos_ref, sin_ref,
                    u_ref, q_ref, k_ref, vt_ref):
    h = _rms(x_ref[...], g_ref[...]).astype(BF16)
    proj = jnp.dot(h, win_ref[...], preferred_element_type=F32)
    o_cq = S5_WIDTH
    o_ckv = o_cq + MLA_Q_LORA
    o_kr = o_ckv + MLA_KV_LORA
    u_ref[...] = proj[:, :o_cq]
    cq = proj[:, o_cq:o_ckv]
    ckv = proj[:, o_ckv:o_kr]
    kr = proj[:, o_kr:o_kr + LANES]
    cos = cos_ref[...]
    sin = sin_ref[...]
    half = MLA_ROPE // 2
    scale = (MLA_NOPE + MLA_ROPE) ** -0.5 * LOG2E
    q = jnp.dot(_rms(cq, qn_ref[...]).astype(BF16), wuq_ref[...], preferred_element_type=F32)
    ckn = _rms(ckv, kvn_ref[...]).astype(BF16)
    kn = jnp.dot(ckn, wuk_ref[...], preferred_element_type=F32)
    vt_ref[0, 0] = lax.dot_general(wuvt_ref[...], ckn, _NT, preferred_element_type=F32).astype(BF16)
    kpe = _rope(kr, cos, sin, half, MLA_NOPE, LANES)
    for hd in range(MLA_HEADS):
        sl = slice(hd * LANES, (hd + 1) * LANES)
        q_ref[:, sl] = (_rope(q[:, sl], cos, sin, half, MLA_NOPE, LANES) * scale).astype(BF16)
        k_ref[:, sl] = (kn[:, sl] + kpe).astype(BF16)


def _vt_spec(nblk, rows):
    return pl.BlockSpec((1, 1, rows, PREP_TM), lambda i: (i // nblk, i % nblk, 0, 0))


def _even_prep(x, g, win, qn, wuq, kvn, wuk, wuvt, cos, sin, nb):
    t = x.shape[0]
    tm = PREP_TM
    nblk = t // nb // tm
    row = lambda w: pl.BlockSpec((tm, w), lambda i: (i, 0))
    return pl.pallas_call(
        _even_prep_body,
        grid=(t // tm,),
        in_specs=[row(D_MODEL), _const_spec((1, D_MODEL)), _const_spec(win.shape),
                  _const_spec((1, MLA_Q_LORA)), _const_spec(wuq.shape),
                  _const_spec((1, MLA_KV_LORA)), _const_spec(wuk.shape), _const_spec(wuvt.shape),
                  row(LANES), row(LANES)],
        out_specs=[row(S5_WIDTH), row(MLA_HEADS * LANES), row(MLA_HEADS * LANES),
                   _vt_spec(nblk, MLA_HEADS * MLA_V)],
        out_shape=[jax.ShapeDtypeStruct((t, S5_WIDTH), F32),
                   jax.ShapeDtypeStruct((t, MLA_HEADS * LANES), BF16),
                   jax.ShapeDtypeStruct((t, MLA_HEADS * LANES), BF16),
                   jax.ShapeDtypeStruct((nb, nblk, MLA_HEADS * MLA_V, tm), BF16)],
        compiler_params=_params(("parallel",)),
        name="even_prep",
    )(x, g.reshape(1, -1), win, qn.reshape(1, -1), wuq, kvn.reshape(1, -1), wuk, wuvt, cos, sin)


def _odd_prep_body(x_ref, g_ref, wqk_ref, wvt_ref, cos_ref, sin_ref, q_ref, k_ref, vt_ref):
    h = _rms(x_ref[...], g_ref[...]).astype(BF16)
    cos = cos_ref[...]
    sin = sin_ref[...]
    half = DIFF_ROT // 2
    width = DIFF_HEADS * 2 * DIFF_HEAD_DIM
    scale = DIFF_HEAD_DIM ** -0.5 * LOG2E
    q = jnp.dot(h, wqk_ref[:, :width], preferred_element_type=F32)
    k = jnp.dot(h, wqk_ref[:, width:], preferred_element_type=F32)
    vt_ref[0, 0] = lax.dot_general(wvt_ref[...], h, _NT, preferred_element_type=F32).astype(BF16)
    for c in range(width // LANES):
        sl = slice(c * LANES, (c + 1) * LANES)
        q_ref[:, sl] = (_rope(q[:, sl], cos, sin, half, 0, DIFF_HEAD_DIM) * scale).astype(BF16)
        k_ref[:, sl] = _rope(k[:, sl], cos, sin, half, 0, DIFF_HEAD_DIM).astype(BF16)


def _odd_prep(x, g, wqk, wvt, cos, sin, nb):
    t = x.shape[0]
    tm = PREP_TM
    nblk = t // nb // tm
    width = DIFF_HEADS * 2 * DIFF_HEAD_DIM
    row = lambda w: pl.BlockSpec((tm, w), lambda i: (i, 0))
    return pl.pallas_call(
        _odd_prep_body,
        grid=(t // tm,),
        in_specs=[row(D_MODEL), _const_spec((1, D_MODEL)), _const_spec(wqk.shape), _const_spec(wvt.shape),
                  row(LANES), row(LANES)],
        out_specs=[row(width), row(width), _vt_spec(nblk, width)],
        out_shape=[jax.ShapeDtypeStruct((t, width), BF16), jax.ShapeDtypeStruct((t, width), BF16),
                   jax.ShapeDtypeStruct((nb, nblk, width, tm), BF16)],
        compiler_params=_params(("parallel",)),
        name="odd_prep",
    )(x, g.reshape(1, -1), wqk, wvt, cos, sin)


def _s5_body(u_ref, ar_ref, ai_ref, bre_ref, bim_ref, c_ref, d_ref, wglu_ref, bglu_ref, o_ref,
             xr_ref, xi_ref, sr_ref, si_ref):
    nb = u_ref.shape[0]

    @pl.when(pl.program_id(0) == 0)
    def _():
        sr_ref[...] = jnp.zeros_like(sr_ref)
        si_ref[...] = jnp.zeros_like(si_ref)

    for b in range(nb):
        ub = u_ref[b].astype(BF16)
        for n in range(S5_FEATS // S5_TILE):
            k = n * S5_TILE // S5_EXPAND // S5_TILE
            uk = ub[:, k * S5_TILE:(k + 1) * S5_TILE]
            bur = jnp.dot(uk, bre_ref[n], preferred_element_type=F32)
            bui = jnp.dot(uk, bim_ref[n], preferred_element_type=F32)
            for h in range(S5_TILE // LANES):
                j = n * (S5_TILE // LANES) + h
                xr_ref[b, j * S5_PITCH:j * S5_PITCH + S5_TS, :] = bur[:, h * LANES:(h + 1) * LANES]
                xi_ref[b, j * S5_PITCH:j * S5_PITCH + S5_TS, :] = bui[:, h * LANES:(h + 1) * LANES]

    ar = ar_ref[...]
    ai = ai_ref[...]

    def step(t, carry):
        new = []
        for b in range(nb):
            pr, pi = carry[2 * b], carry[2 * b + 1]
            idx = pl.ds(t, S5_ROWS, stride=S5_PITCH)
            nr = ar * pr - ai * pi + xr_ref[b, idx, :]
            ni = ar * pi + ai * pr + xi_ref[b, idx, :]
            xr_ref[b, idx, :] = nr
            xi_ref[b, idx, :] = ni
            new += [nr, ni]
        return tuple(new)

    init = []
    for b in range(nb):
        init += [sr_ref[b], si_ref[b]]
    fin = lax.fori_loop(0, S5_TS, step, tuple(init), unroll=4)
    for b in range(nb):
        sr_ref[b] = fin[2 * b]
        si_ref[b] = fin[2 * b + 1]

    per_tile = S5_ROWS * S5_TILE // S5_WIDTH
    for b in range(nb):
        ys = []
        for k in range(S5_WIDTH // S5_TILE):
            slabs = range(k * per_tile, (k + 1) * per_tile)
            parts = [xr_ref[b, j * S5_PITCH:j * S5_PITCH + S5_TS, :].astype(BF16) for j in slabs]
            parts += [xi_ref[b, j * S5_PITCH:j * S5_PITCH + S5_TS, :].astype(BF16) for j in slabs]
            ys.append(jnp.dot(jnp.concatenate(parts, axis=1), c_ref[k], preferred_element_type=F32))
        y = jnp.concatenate(ys, axis=1) + d_ref[...] * u_ref[b]
        g = 0.5 * y * (1.0 + jnp.tanh(math.sqrt(2.0 / math.pi) * (y + 0.044715 * (y * y * y))))
        z = jnp.dot(g.astype(BF16), wglu_ref[...], preferred_element_type=F32) + bglu_ref[...]
        o_ref[b] = (g * jax.nn.sigmoid(z)).astype(BF16)


def _s5(u, ar, ai, bre, bim, cmat, dskip, wglu, bglu):
    nb, s, _ = u.shape
    blk = pl.BlockSpec((nb, S5_TS, S5_WIDTH), lambda i: (0, i, 0))
    return pl.pallas_call(
        _s5_body,
        grid=(s // S5_TS,),
        in_specs=[blk, _const_spec(ar.shape), _const_spec(ai.shape), _const_spec(bre.shape),
                  _const_spec(bim.shape), _const_spec(cmat.shape), _const_spec((1, S5_WIDTH)),
                  _const_spec(wglu.shape), _const_spec((1, S5_WIDTH))],
        out_specs=blk,
        out_shape=jax.ShapeDtypeStruct((nb, s, S5_WIDTH), BF16),
        scratch_shapes=[pltpu.VMEM((nb, S5_ROWS * S5_PITCH, LANES), F32),
                        pltpu.VMEM((nb, S5_ROWS * S5_PITCH, LANES), F32),
                        pltpu.VMEM((nb, S5_ROWS, LANES), F32),
                        pltpu.VMEM((nb, S5_ROWS, LANES), F32)],
        compiler_params=_params(("arbitrary",)),
        name="s5",
    )(u, ar, ai, bre, bim, cmat, dskip.reshape(1, -1), wglu, bglu.reshape(1, -1))


def _flash_t(k_ref, vt_ref, qt_ref, qi, scr, k_block, v_block):
    s0, s1, p0, p1, a0, a1, x0, x1, m_ref, acc_ref = scr
    strips = tuple(range(ATT_LANES // ATT_STRIP))

    def q_off(c):
        return (c * ATT_STRIP) % ATT_TQ

    lower = tuple(c for c in strips if q_off(c) < ATT_TK)
    upper = tuple(c for c in strips if q_off(c) >= ATT_TK)
    m_ref[...] = jnp.full(m_ref.shape, -jnp.inf, F32)
    acc_ref[...] = jnp.zeros(acc_ref.shape, F32)
    p1[...] = jnp.zeros(p1.shape, jnp.uint32)
    a1[...] = jnp.ones(a1.shape, F32)
    ones = jnp.ones((ATT_SUM_ROWS, ATT_TK), BF16)

    def scores(j, s_ref, x_ref, which):
        for c in which:
            sl = slice(c * ATT_STRIP, (c + 1) * ATT_STRIP)
            kb = k_ref[0, pl.ds(j * ATT_TK, ATT_TK), k_block(c) * LANES:(k_block(c) + 1) * LANES]
            s = jnp.dot(kb, qt_ref[:, sl], preferred_element_type=F32)
            s_ref[:, sl] = s
            x_ref[:, sl] = jnp.max(s, axis=0, keepdims=True)

    def softmax(s_ref, x_ref, p_ref, a_ref, which, key_off):
        for c in which:
            for t in range(ATT_STRIP // LANES):
                sl = slice(c * ATT_STRIP + t * LANES, c * ATT_STRIP + (t + 1) * LANES)
                s = s_ref[:, sl]
                if key_off is None:
                    s_max = x_ref[:, sl]
                else:
                    kpos = lax.broadcasted_iota(jnp.int32, s.shape, 0) + key_off
                    qpos = lax.broadcasted_iota(jnp.int32, s.shape, 1) + (q_off(c) + t * LANES)
                    s = jnp.where(kpos <= qpos, s, -jnp.inf)
                    s_max = jnp.max(s, axis=0, keepdims=True)
                m_old = m_ref[:, sl]
                m_new = jnp.maximum(m_old, s_max)
                a_ref[:, sl] = jnp.exp2(m_old - m_new)
                m_ref[:, sl] = m_new
                p_ref[:, sl] = pltpu.bitcast(jnp.exp2(s - m_new).astype(BF16), jnp.uint32)

    def value(pair, half, p_ref, a_ref, which):
        for c in which:
            sl = slice(c * ATT_STRIP, (c + 1) * ATT_STRIP)
            vt = vt_ref[0, pair, v_block(c) * LANES:(v_block(c) + 1) * LANES, half * ATT_TK:(half + 1) * ATT_TK]
            p = pltpu.bitcast(p_ref[:, sl], BF16)
            pv = jnp.dot(jnp.concatenate([vt, ones], axis=0), p, preferred_element_type=F32)
            acc_ref[:, sl] = a_ref[:, sl] * acc_ref[:, sl] + pv

    scores(0, s0, x0, strips)

    def pair_step(i, carry):
        softmax(s0, x0, p0, a0, strips, None)
        scores(2 * i + 1, s1, x1, strips)
        value(jnp.maximum(i - 1, 0), 1, p1, a1, strips)
        softmax(s1, x1, p1, a1, strips, None)
        scores(2 * i + 2, s0, x0, strips)
        value(i, 0, p0, a0, strips)
        return carry

    lax.fori_loop(0, qi, pair_step, 0)
    scores(2 * qi + 1, s1, x1, upper)
    softmax(s0, x0, p0, a0, lower, 0)
    softmax(s0, x0, p0, a0, upper, None)
    value(jnp.maximum(qi - 1, 0), 1, p1, a1, strips)
    softmax(s1, x1, p1, a1, upper, ATT_TK)
    value(qi, 0, p0, a0, strips)
    value(qi, 1, p1, a1, upper)
    return acc_ref[:LANES, :ATT_LANES] / acc_ref[LANES:LANES + 1, :ATT_LANES]


def _mla_attn_body(q_ref, k_ref, vt_ref, o_ref, *scr):
    qi = pl.program_id(2)
    qt_ref = scr[0]
    heads = ATT_LANES // ATT_TQ
    for h in range(heads):
        qt_ref[:, h * ATT_TQ:(h + 1) * ATT_TQ] = q_ref[0, :, h * LANES:(h + 1) * LANES].astype(F32).T.astype(BF16)
    per_head = ATT_TQ // ATT_STRIP
    ot = _flash_t(k_ref, vt_ref, qt_ref, qi, scr[1:], lambda c: c // per_head, lambda c: c // (2 * per_head))
    row = lax.broadcasted_iota(jnp.int32, (LANES, ATT_TQ), 0)
    for r in range(heads // 2):
        lo = ot[:, 2 * r * ATT_TQ:(2 * r + 1) * ATT_TQ]
        hi = ot[:, (2 * r + 1) * ATT_TQ:(2 * r + 2) * ATT_TQ]
        o_ref[0, :, r * LANES:(r + 1) * LANES] = jnp.where(row < MLA_V, lo, hi).T.astype(BF16)


def _attn_specs(s, qk_width, v_width):
    n = ATT_LANES
    row = pltpu.VMEM((1, n), F32)
    return dict(
        in_specs=[pl.BlockSpec((1, ATT_TQ, qk_width), lambda b, h, i: (b, i, h)),
                  pl.BlockSpec((1, s, qk_width), lambda b, h, i: (b, 0, h)),
                  pl.BlockSpec((1, s // ATT_TQ, v_width, ATT_TQ), lambda b, h, i: (b, 0, h, 0))],
        out_specs=pl.BlockSpec((1, ATT_TQ, v_width), lambda b, h, i: (b, i, h)),
        scratch_shapes=[pltpu.VMEM((LANES, n), BF16),
                        pltpu.VMEM((ATT_TK, n + ATT_PAD), F32), pltpu.VMEM((ATT_TK, n + ATT_PAD), F32),
                        pltpu.VMEM((ATT_TK // 2, n + ATT_PAD), jnp.uint32),
                        pltpu.VMEM((ATT_TK // 2, n + ATT_PAD), jnp.uint32),
                        row, row, row, row, row, pltpu.VMEM((LANES + ATT_SUM_ROWS, n + ATT_PAD), F32)],
        compiler_params=_params(("parallel", "parallel", "arbitrary")))


def _mla_attn(q, k, vt):
    nb, s, _ = q.shape
    heads = ATT_LANES // ATT_TQ
    return pl.pallas_call(
        _mla_attn_body,
        grid=(nb, MLA_HEADS // heads, s // ATT_TQ),
        out_shape=jax.ShapeDtypeStruct((nb, s, MLA_HEADS * MLA_V), BF16),
        name="mla_attn", **_attn_specs(s, heads * LANES, heads * MLA_V),
    )(q, k, vt)


def _diff_attn_body(lam_ref, q_ref, k_ref, vt_ref, sub_ref, o_ref, *scr, out_scale):
    qi = pl.program_id(2)
    qt_ref = scr[0]
    heads = ATT_LANES // (2 * ATT_TQ)
    for g in range(heads):
        qt = q_ref[0, :, g * LANES:(g + 1) * LANES].astype(F32).T
        row = lax.broadcasted_iota(jnp.int32, qt.shape, 0)
        zero = jnp.zeros_like(qt)
        qt_ref[:, 2 * g * ATT_TQ:(2 * g + 1) * ATT_TQ] = jnp.where(row < DIFF_HEAD_DIM, qt, zero).astype(BF16)
        qt_ref[:, (2 * g + 1) * ATT_TQ:(2 * g + 2) * ATT_TQ] = jnp.where(row < DIFF_HEAD_DIM, zero, qt).astype(BF16)
    per_head = 2 * ATT_TQ // ATT_STRIP
    ot = _flash_t(k_ref, vt_ref, qt_ref, qi, scr[1:], lambda c: c // per_head, lambda c: c // per_head)
    for g in range(heads):
        o = (ot[:, 2 * g * ATT_TQ:(2 * g + 1) * ATT_TQ]
             - lam_ref[0, 0] * ot[:, (2 * g + 1) * ATT_TQ:(2 * g + 2) * ATT_TQ]).T
        o_ref[0, :, g * LANES:(g + 1) * LANES] = (_rms(o, sub_ref[...]) * out_scale).astype(BF16)


def _diff_attn(lam, q, k, vt, subln, out_scale):
    nb, s, _ = q.shape
    heads = ATT_LANES // (2 * ATT_TQ)
    spec = _attn_specs(s, heads * LANES, heads * LANES)
    spec["in_specs"] = ([pl.BlockSpec(memory_space=pltpu.SMEM)] + spec["in_specs"]
                        + [pl.BlockSpec((1, LANES), lambda b, h, i: (0, 0))])
    return pl.pallas_call(
        functools.partial(_diff_attn_body, out_scale=out_scale),
        grid=(nb, DIFF_HEADS // heads, s // ATT_TQ),
        out_shape=jax.ShapeDtypeStruct((nb, s, DIFF_HEADS * 2 * DIFF_HEAD_DIM), BF16),
        name="diff_attn", **spec,
    )(lam, q, k, vt, subln.reshape(1, -1))


def _rope_table(positions, rot_dim, theta, first_lo, period):
    half = rot_dim // 2
    inv = theta ** (-jnp.arange(half, dtype=F32) * 2.0 / rot_dim)
    ang = inv[:, None] * positions.astype(F32).reshape(1, -1)
    cos, sin = jnp.cos(ang), jnp.sin(ang)
    rel = np.arange(LANES) % period - first_lo
    rotated = ((rel >= 0) & (rel < rot_dim))[:, None]
    freq = np.where(rotated[:, 0], rel % half, 0)
    sign = np.where(rel < half, -1.0, 1.0).astype(np.float32)[:, None]
    cos_t = jnp.where(rotated, cos[freq], 1.0)
    sin_t = jnp.where(rotated, sin[freq] * sign, 0.0)
    return cos_t.T, sin_t.T


def _s5_discretize(lam_re, lam_im, log_dt, b_re, b_im):
    dt = jnp.exp(log_dt)[:, None]
    mag = jnp.exp(lam_re * dt)
    ang = lam_im * dt
    ab_re = mag * jnp.cos(ang)
    ab_im = mag * jnp.sin(ang)
    den = lam_re * lam_re + lam_im * lam_im
    nr = ab_re - 1.0
    f_re = (nr * lam_re + ab_im * lam_im) / den
    f_im = (ab_im * lam_re - nr * lam_im) / den
    bb_re = f_re[..., None] * b_re - f_im[..., None] * b_im
    bb_im = f_re[..., None] * b_im + f_im[..., None] * b_re
    return ab_re, ab_im, bb_re, bb_im


def _block_diag_in(bb):
    eye = jnp.eye(S5_GROUPS, dtype=F32)
    return jnp.einsum('gph,gk->ghkp', bb, eye).reshape(S5_WIDTH, S5_FEATS)


def _block_diag_out(c):
    eye = jnp.eye(S5_GROUPS, dtype=F32)
    return jnp.einsum('ghp,gk->gpkh', c, eye).reshape(S5_FEATS, S5_WIDTH)


def _s5_in_tiles(bb):
    full = _block_diag_in(bb)
    tiles = [full[(n * S5_TILE // S5_EXPAND // S5_TILE) * S5_TILE:, n * S5_TILE:(n + 1) * S5_TILE][:S5_TILE]
             for n in range(S5_FEATS // S5_TILE)]
    return jnp.stack(tiles).astype(BF16)


def _s5_out_tiles(c_re, c_im):
    full_re, full_im = _block_diag_out(c_re), -_block_diag_out(c_im)
    rows = S5_TILE * S5_EXPAND
    tiles = [jnp.concatenate([full_re[k * rows:(k + 1) * rows, k * S5_TILE:(k + 1) * S5_TILE],
                              full_im[k * rows:(k + 1) * rows, k * S5_TILE:(k + 1) * S5_TILE]], axis=0)
             for k in range(S5_WIDTH // S5_TILE)]
    return jnp.stack(tiles).astype(BF16)


def _pad_heads(w, heads, width, lo, hi):
    kdim = w.shape[0]
    w = w.reshape(kdim, heads, width)[:, :, lo:hi]
    w = jnp.pad(w, ((0, 0), (0, 0), (0, LANES - (hi - lo))))
    return w.reshape(kdim, heads * LANES)


def kernel(x, positions, ffn_norm, ffn_w_gu, ffn_w_down, ev_norm, ev_w_in, s5_lambda_re, s5_lambda_im, s5_log_dt, s5_b_re, s5_b_im, s5_c_re, s5_c_im, s5_d, s5_w_glu, s5_b_glu, mla_q_norm, mla_w_uq, mla_kv_norm, mla_w_ukv, ev_w_out, od_norm, od_w_in, diff_lq1, diff_lk1, diff_lq2, diff_lk2, diff_subln, od_w_out, final_norm):
    nb, s, d = x.shape
    t = nb * s
    xt = x.reshape(t, d)
    wgu = ffn_w_gu.astype(BF16)
    wdn = ffn_w_down.astype(BF16)

    xt = _ffn(xt, [], ffn_norm[0, 0], wgu, wdn, 0, 0)

    cos_m, sin_m = _rope_table(positions, MLA_ROPE, MLA_ROPE_THETA, MLA_NOPE, LANES)
    w_in = ev_w_in[0]
    o_kr = S5_WIDTH + MLA_Q_LORA + MLA_KV_LORA
    w_kr = jnp.pad(w_in[:, o_kr:], ((0, 0), (MLA_NOPE, LANES - MLA_NOPE - MLA_ROPE)))
    win_p = jnp.concatenate([w_in[:, :o_kr], w_kr], axis=1).astype(BF16)
    wuq_p = _pad_heads(mla_w_uq[0], MLA_HEADS, MLA_NOPE + MLA_ROPE, 0, MLA_NOPE + MLA_ROPE).astype(BF16)
    wuk_p = _pad_heads(mla_w_ukv[0], MLA_HEADS, MLA_NOPE + MLA_V, 0, MLA_NOPE).astype(BF16)
    wuv = mla_w_ukv[0].reshape(MLA_KV_LORA, MLA_HEADS, MLA_NOPE + MLA_V)[:, :, MLA_NOPE:]
    wuvt = wuv.reshape(MLA_KV_LORA, MLA_HEADS * MLA_V).T.astype(BF16)
    u, q, k, vt = _even_prep(xt, ev_norm[0], win_p, mla_q_norm[0], wuq_p, mla_kv_norm[0], wuk_p, wuvt,
                             cos_m, sin_m, nb)

    ab_re, ab_im, bb_re, bb_im = _s5_discretize(s5_lambda_re[0], s5_lambda_im[0], s5_log_dt[0], s5_b_re[0], s5_b_im[0])
    s5_out = _s5(u.reshape(nb, s, S5_WIDTH), ab_re.reshape(S5_ROWS, LANES), ab_im.reshape(S5_ROWS, LANES),
                 _s5_in_tiles(bb_re), _s5_in_tiles(bb_im), _s5_out_tiles(s5_c_re[0], s5_c_im[0]),
                 s5_d[0].reshape(-1), s5_w_glu[0].astype(BF16), s5_b_glu[0])

    o = _mla_attn(q.reshape(nb, s, -1), k.reshape(nb, s, -1), vt)
    w_out = ev_w_out[0].astype(BF16)
    xt = _ffn(xt, [(s5_out.reshape(t, S5_WIDTH), w_out[:S5_WIDTH]), (o.reshape(t, -1), w_out[S5_WIDTH:])],
              ffn_norm[0, 1], wgu, wdn, 0, 1)

    xt = _ffn(xt, [], ffn_norm[1, 0], wgu, wdn, 1, 0)
    cos_t, sin_t = _rope_table(positions, DIFF_ROT, ROPE_THETA, 0, DIFF_HEAD_DIM)
    width = DIFF_HEADS * 2 * DIFF_HEAD_DIM
    w_in = od_w_in[0]
    q, k, vt = _odd_prep(xt, od_norm[0], w_in[:, :2 * width].astype(BF16), w_in[:, 2 * width:].T.astype(BF16),
                         cos_t, sin_t, nb)
    lam_init = 0.8 - 0.6 * math.exp(-0.3 * 1)
    lam = (jnp.exp(jnp.sum(diff_lq1[0] * diff_lk1[0])) - jnp.exp(jnp.sum(diff_lq2[0] * diff_lk2[0])) + lam_init)
    o = _diff_attn(lam.reshape(1, 1), q.reshape(nb, s, -1), k.reshape(nb, s, -1), vt, diff_subln[0],
                   1.0 - lam_init)
    xt = _ffn(xt, [(o.reshape(t, -1), od_w_out[0].astype(BF16))], ffn_norm[1, 1], wgu, wdn, 1, 1,
              gfinal=final_norm)
    return xt.reshape(nb, s, d)
```

```python
import functools
import math

import jax
import jax.numpy as jnp
import numpy as np
from jax import lax
from jax.experimental import pallas as pl
from jax.experimental.pallas import tpu as pltpu

F32 = jnp.float32
BF16 = jnp.bfloat16

D_MODEL = 1024
NORM_EPS = 1e-6
D_FF = 2816
S5_WIDTH = 512
S5_GROUP_CH = 16
S5_GROUPS = 32
S5_STATE = 64
S5_FEATS = S5_GROUPS * S5_STATE
MLA_HEADS = 8
MLA_NOPE = 64
MLA_ROPE = 32
MLA_V = 64
MLA_Q_LORA = 384
MLA_KV_LORA = 256
MLA_ROPE_THETA = 10000.0
DIFF_HEADS = 8
DIFF_HEAD_DIM = 64
DIFF_ROT = 16
ROPE_THETA = 500000.0

LANES = 128
SUBLANES = 8
VMEM_LIMIT = 56 * 1024 * 1024
LOG2E = math.log2(math.e)

FFN_TM = 512
FFN_CHUNKS = 1
S5_TS = 512
S5_PITCH = S5_TS + 4
S5_ROWS = S5_FEATS // LANES
S5_TILE = 256
S5_EXPAND = S5_FEATS // S5_WIDTH
ATT_TQ = 1024
ATT_TK = ATT_TQ // 2
ATT_LANES = 2 * ATT_TQ
ATT_STRIP = 256
ATT_SUM_ROWS = 16
ATT_PAD = LANES
PREP_TM = ATT_TQ

_NT = (((1,), (1,)), ((), ()))


def _rms(x, g):
    ms = jnp.mean(x * x, axis=-1, keepdims=True)
    return x * lax.rsqrt(ms + NORM_EPS) * g


def _const_spec(shape):
    nd = len(shape)
    return pl.BlockSpec(shape, lambda *_: (0,) * nd, pipeline_mode=pl.Buffered(1))


def _params(sem, flags=None):
    return pltpu.CompilerParams(dimension_semantics=sem, vmem_limit_bytes=VMEM_LIMIT, flags=flags)


def _ffn_body(*refs, n_mix, final):
    x_ref = refs[0]
    mix_refs = refs[1:1 + 2 * n_mix]
    g_ref, wgu_ref, wd_ref = refs[1 + 2 * n_mix:4 + 2 * n_mix]
    gf_ref = refs[4 + 2 * n_mix] if final else None
    o_ref = refs[-1]

    x = x_ref[...]
    for i in range(n_mix):
        x = x + jnp.dot(mix_refs[2 * i][...], mix_refs[2 * i + 1][...], preferred_element_type=F32)
    h = _rms(x, g_ref[...]).astype(BF16)
    ck = D_FF // FFN_CHUNKS
    y = None
    for c in range(FFN_CHUNKS):
        g = jnp.dot(h, wgu_ref[:, c * ck:(c + 1) * ck], preferred_element_type=F32)
        u = jnp.dot(h, wgu_ref[:, D_FF + c * ck:D_FF + (c + 1) * ck], preferred_element_type=F32)
        a = (g * jax.nn.sigmoid(g) * u).astype(BF16)
        d = jnp.dot(a, wd_ref[c * ck:(c + 1) * ck, :], preferred_element_type=F32)
        y = d if y is None else y + d
    out = x + 0.5 * y
    if final:
        out = _rms(out, gf_ref[...])
    o_ref[...] = out


def _ffn(x, mixes, g, wgu, wd, layer, pos, gfinal=None):
    t = x.shape[0]
    n_mix = len(mixes)
    final = gfinal is not None
    in_specs = [pl.BlockSpec((FFN_TM, D_MODEL), lambda i: (i, 0))]
    args = [x]
    for m, w in mixes:
        in_specs += [pl.BlockSpec((FFN_TM, m.shape[1]), lambda i: (i, 0)), _const_spec(w.shape)]
        args += [m, w]

    def slab(w):
        return pl.BlockSpec((None, None) + w.shape[2:], lambda i: (layer, pos, 0, 0), pipeline_mode=pl.Buffered(1))

    in_specs += [_const_spec((1, D_MODEL)), slab(wgu), slab(wd)]
    args += [g.reshape(1, D_MODEL), wgu, wd]
    if final:
        in_specs.append(_const_spec((1, D_MODEL)))
        args.append(gfinal.reshape(1, D_MODEL))
    return pl.pallas_call(
        functools.partial(_ffn_body, n_mix=n_mix, final=final),
        grid=(t // FFN_TM,),
        in_specs=in_specs,
        out_specs=pl.BlockSpec((FFN_TM, D_MODEL), lambda i: (i, 0)),
        out_shape=jax.ShapeDtypeStruct((t, D_MODEL), F32),
        compiler_params=_params(("parallel",)),
        name="ffn",
    )(*args)


def _rope(blk, cos, sin, half, first_lo, period):
    lane = lax.broadcasted_iota(jnp.int32, blk.shape, 1) % period
    fwd = pltpu.roll(blk, LANES - half, 1)
    bwd = pltpu.roll(blk, half, 1)
    rot = jnp.where(lane < first_lo + half, fwd, bwd)
    return blk * cos + rot * sin


def _rope_dup(blk, cos, sin, half):
    return blk * cos + pltpu.roll(blk, LANES - half, 1) * sin


def _even_prep_body(x_ref, g_ref, win_ref, qn_ref, wuq_ref, kvn_ref, wuk_ref, wuvt_ref, cos_ref, sin_ref,
                    u_ref, q_ref, k_ref, vt_ref):
    h = _rms(x_ref[...], g_ref[...]).astype(BF16)
    proj = jnp.dot(h, win_ref[...], preferred_element_type=F32)
    o_cq = S5_WIDTH
    o_ckv = o_cq + MLA_Q_LORA
    o_kr = o_ckv + MLA_KV_LORA
    u_ref[...] = proj[:, :o_cq]
    cq = proj[:, o_cq:o_ckv]
    ckv = proj[:, o_ckv:o_kr]
    kr = proj[:, o_kr:o_kr + LANES]
    cos = cos_ref[...]
    sin = sin_ref[...]
    half = MLA_ROPE // 2
    scale = (MLA_NOPE + MLA_ROPE) ** -0.5 * LOG2E
    q = jnp.dot(_rms(cq, qn_ref[...]).astype(BF16), wuq_ref[...], preferred_element_type=F32)
    ckn = _rms(ckv, kvn_ref[...]).astype(BF16)
    kn = jnp.dot(ckn, wuk_ref[...], preferred_element_type=F32)
    vt_ref[0, 0] = lax.dot_general(wuvt_ref[...], ckn, _NT, preferred_element_type=F32).astype(BF16)
    kpe = _rope_dup(kr, cos, sin, half)
    for hd in range(MLA_HEADS):
        sl = slice(hd * LANES, (hd + 1) * LANES)
        q_ref[:, sl] = (_rope_dup(q[:, sl], cos, sin, half) * scale).astype(BF16)
        k_ref[:, sl] = (kn[:, sl] + kpe).astype(BF16)


def _vt_spec(nblk, rows):
    return pl.BlockSpec((1, 1, rows, PREP_TM), lambda i: (i // nblk, i % nblk, 0, 0))


def _even_prep(x, g, win, qn, wuq, kvn, wuk, wuvt, cos, sin, nb):
    t = x.shape[0]
    tm = PREP_TM
    nblk = t // nb // tm
    row = lambda w: pl.BlockSpec((tm, w), lambda i: (i, 0))
    return pl.pallas_call(
        _even_prep_body,
        grid=(t // tm,),
        in_specs=[row(D_MODEL), _const_spec((1, D_MODEL)), _const_spec(win.shape),
                  _const_spec((1, MLA_Q_LORA)), _const_spec(wuq.shape),
                  _const_spec((1, MLA_KV_LORA)), _const_spec(wuk.shape), _const_spec(wuvt.shape),
                  row(LANES), row(LANES)],
        out_specs=[row(S5_WIDTH), row(MLA_HEADS * LANES), row(MLA_HEADS * LANES),
                   _vt_spec(nblk, MLA_HEADS * MLA_V)],
        out_shape=[jax.ShapeDtypeStruct((t, S5_WIDTH), F32),
                   jax.ShapeDtypeStruct((t, MLA_HEADS * LANES), BF16),
                   jax.ShapeDtypeStruct((t, MLA_HEADS * LANES), BF16),
                   jax.ShapeDtypeStruct((nb, nblk, MLA_HEADS * MLA_V, tm), BF16)],
        compiler_params=_params(("parallel",)),
        name="even_prep",
    )(x, g.reshape(1, -1), win, qn.reshape(1, -1), wuq, kvn.reshape(1, -1), wuk, wuvt, cos, sin)


def _odd_prep_body(x_ref, g_ref, wqk_ref, wvt_ref, cos_ref, sin_ref, q_ref, k_ref, vt_ref):
    h = _rms(x_ref[...], g_ref[...]).astype(BF16)
    cos = cos_ref[...]
    sin = sin_ref[...]
    half = DIFF_ROT // 2
    width = DIFF_HEADS * 2 * DIFF_HEAD_DIM
    scale = DIFF_HEAD_DIM ** -0.5 * LOG2E
    q = jnp.dot(h, wqk_ref[:, :width], preferred_element_type=F32)
    k = jnp.dot(h, wqk_ref[:, width:], preferred_element_type=F32)
    vt_ref[0, 0] = lax.dot_general(wvt_ref[...], h, _NT, preferred_element_type=F32).astype(BF16)
    for c in range(width // LANES):
        sl = slice(c * LANES, (c + 1) * LANES)
        q_ref[:, sl] = (_rope(q[:, sl], cos, sin, half, 0, DIFF_HEAD_DIM) * scale).astype(BF16)
        k_ref[:, sl] = _rope(k[:, sl], cos, sin, half, 0, DIFF_HEAD_DIM).astype(BF16)


def _odd_prep(x, g, wqk, wvt, cos, sin, nb):
    t = x.shape[0]
    tm = PREP_TM
    nblk = t // nb // tm
    width = DIFF_HEADS * 2 * DIFF_HEAD_DIM
    row = lambda w: pl.BlockSpec((tm, w), lambda i: (i, 0))
    return pl.pallas_call(
        _odd_prep_body,
        grid=(t // tm,),
        in_specs=[row(D_MODEL), _const_spec((1, D_MODEL)), _const_spec(wqk.shape), _const_spec(wvt.shape),
                  row(LANES), row(LANES)],
        out_specs=[row(width), row(width), _vt_spec(nblk, width)],
        out_shape=[jax.ShapeDtypeStruct((t, width), BF16), jax.ShapeDtypeStruct((t, width), BF16),
                   jax.ShapeDtypeStruct((nb, nblk, width, tm), BF16)],
        compiler_params=_params(("parallel",)),
        name="odd_prep",
    )(x, g.reshape(1, -1), wqk, wvt, cos, sin)


def _s5_body(u_ref, ar_ref, ai_ref, bre_ref, bim_ref, c_ref, d_ref, wglu_ref, bglu_ref, o_ref,
             xr_ref, xi_ref, sr_ref, si_ref):
    nb = u_ref.shape[0]

    @pl.when(pl.program_id(0) == 0)
    def _():
        sr_ref[...] = jnp.zeros_like(sr_ref)
        si_ref[...] = jnp.zeros_like(si_ref)

    for b in range(nb):
        ub = u_ref[b].astype(BF16)
        for n in range(S5_FEATS // S5_TILE):
            k = n * S5_TILE // S5_EXPAND // S5_TILE
            uk = ub[:, k * S5_TILE:(k + 1) * S5_TILE]
            bur = jnp.dot(uk, bre_ref[n], preferred_element_type=F32)
            bui = jnp.dot(uk, bim_ref[n], preferred_element_type=F32)
            for h in range(S5_TILE // LANES):
                j = n * (S5_TILE // LANES) + h
                xr_ref[b, j * S5_PITCH:j * S5_PITCH + S5_TS, :] = bur[:, h * LANES:(h + 1) * LANES]
                xi_ref[b, j * S5_PITCH:j * S5_PITCH + S5_TS, :] = bui[:, h * LANES:(h + 1) * LANES]

    ar = ar_ref[...]
    ai = ai_ref[...]

    def step(t, carry):
        new = []
        for b in range(nb):
            pr, pi = carry[2 * b], carry[2 * b + 1]
            idx = pl.ds(t, S5_ROWS, stride=S5_PITCH)
            nr = ar * pr - ai * pi + xr_ref[b, idx, :]
            ni = ar * pi + ai * pr + xi_ref[b, idx, :]
            xr_ref[b, idx, :] = nr
            xi_ref[b, idx, :] = ni
            new += [nr, ni]
        return tuple(new)

    init = []
    for b in range(nb):
        init += [sr_ref[b], si_ref[b]]
    fin = lax.fori_loop(0, S5_TS, step, tuple(init), unroll=8)
    for b in range(nb):
        sr_ref[b] = fin[2 * b]
        si_ref[b] = fin[2 * b + 1]

    per_tile = S5_ROWS * S5_TILE // S5_WIDTH
    for b in range(nb):
        ys = []
        for k in range(S5_WIDTH // S5_TILE):
            slabs = range(k * per_tile, (k + 1) * per_tile)
            parts = [xr_ref[b, j * S5_PITCH:j * S5_PITCH + S5_TS, :].astype(BF16) for j in slabs]
            parts += [xi_ref[b, j * S5_PITCH:j * S5_PITCH + S5_TS, :].astype(BF16) for j in slabs]
            ys.append(jnp.dot(jnp.concatenate(parts, axis=1), c_ref[k], preferred_element_type=F32))
        y = jnp.concatenate(ys, axis=1) + d_ref[...] * u_ref[b]
        g = 0.5 * y * (1.0 + jnp.tanh(math.sqrt(2.0 / math.pi) * (y + 0.044715 * (y * y * y))))
        z = jnp.dot(g.astype(BF16), wglu_ref[...], preferred_element_type=F32) + bglu_ref[...]
        o_ref[b] = (g * jax.nn.sigmoid(z)).astype(BF16)


def _s5(u, ar, ai, bre, bim, cmat, dskip, wglu, bglu):
    nb, s, _ = u.shape
    blk = pl.BlockSpec((nb, S5_TS, S5_WIDTH), lambda i: (0, i, 0))
    return pl.pallas_call(
        _s5_body,
        grid=(s // S5_TS,),
        in_specs=[blk, _const_spec(ar.shape), _const_spec(ai.shape), _const_spec(bre.shape),
                  _const_spec(bim.shape), _const_spec(cmat.shape), _const_spec((1, S5_WIDTH)),
                  _const_spec(wglu.shape), _const_spec((1, S5_WIDTH))],
        out_specs=blk,
        out_shape=jax.ShapeDtypeStruct((nb, s, S5_WIDTH), BF16),
        scratch_shapes=[pltpu.VMEM((nb, S5_ROWS * S5_PITCH, LANES), F32),
                        pltpu.VMEM((nb, S5_ROWS * S5_PITCH, LANES), F32),
                        pltpu.VMEM((nb, S5_ROWS, LANES), F32),
                        pltpu.VMEM((nb, S5_ROWS, LANES), F32)],
        compiler_params=_params(("arbitrary",)),
        name="s5",
    )(u, ar, ai, bre, bim, cmat, dskip.reshape(1, -1), wglu, bglu.reshape(1, -1))


def _flash_t(k_ref, vt_ref, qt_ref, qi, scr, k_block, v_block):
    s0, s1, p0, p1, a0, a1, x0, x1, m_ref, acc_ref = scr
    strips = tuple(range(ATT_LANES // ATT_STRIP))

    def q_off(c):
        return (c * ATT_STRIP) % ATT_TQ

    lower = tuple(c for c in strips if q_off(c) < ATT_TK)
    upper = tuple(c for c in strips if q_off(c) >= ATT_TK)
    m_ref[...] = jnp.full(m_ref.shape, -jnp.inf, F32)
    acc_ref[...] = jnp.zeros(acc_ref.shape, F32)
    p1[...] = jnp.zeros(p1.shape, jnp.uint32)
    a1[...] = jnp.ones(a1.shape, F32)
    ones = jnp.ones((ATT_SUM_ROWS, ATT_TK), BF16)

    def scores(j, s_ref, x_ref, which):
        for c in which:
            sl = slice(c * ATT_STRIP, (c + 1) * ATT_STRIP)
            kb = k_ref[0, pl.ds(j * ATT_TK, ATT_TK), k_block(c) * LANES:(k_block(c) + 1) * LANES]
            s = jnp.dot(kb, qt_ref[:, sl], preferred_element_type=F32)
            s_ref[:, sl] = s
            x_ref[:, sl] = jnp.max(s, axis=0, keepdims=True)

    def softmax(s_ref, x_ref, p_ref, a_ref, which, key_off):
        for c in which:
            for t in range(ATT_STRIP // LANES):
                sl = slice(c * ATT_STRIP + t * LANES, c * ATT_STRIP + (t + 1) * LANES)
                s = s_ref[:, sl]
                if key_off is None:
                    s_max = x_ref[:, sl]
                else:
                    kpos = lax.broadcasted_iota(jnp.int32, s.shape, 0) + key_off
                    qpos = lax.broadcasted_iota(jnp.int32, s.shape, 1) + (q_off(c) + t * LANES)
                    s = jnp.where(kpos <= qpos, s, -jnp.inf)
                    s_max = jnp.max(s, axis=0, keepdims=True)
                m_old = m_ref[:, sl]
                m_new = jnp.maximum(m_old, s_max)
                a_ref[:, sl] = jnp.exp2(m_old - m_new)
                m_ref[:, sl] = m_new
                p_ref[:, sl] = pltpu.bitcast(jnp.exp2(s - m_new).astype(BF16), jnp.uint32)

    def value(pair, half, p_ref, a_ref, which):
        for c in which:
            sl = slice(c * ATT_STRIP, (c + 1) * ATT_STRIP)
            vt = vt_ref[0, pair, v_block(c) * LANES:(v_block(c) + 1) * LANES, half * ATT_TK:(half + 1) * ATT_TK]
            p = pltpu.bitcast(p_ref[:, sl], BF16)
            pv = jnp.dot(jnp.concatenate([vt, ones], axis=0), p, preferred_element_type=F32)
            acc_ref[:, sl] = a_ref[:, sl] * acc_ref[:, sl] + pv

    scores(0, s0, x0, strips)

    def pair_step(i, carry):
        softmax(s0, x0, p0, a0, strips, None)
        scores(2 * i + 1, s1, x1, strips)
        value(jnp.maximum(i - 1, 0), 1, p1, a1, strips)
        softmax(s1, x1, p1, a1, strips, None)
        scores(2 * i + 2, s0, x0, strips)
        value(i, 0, p0, a0, strips)
        return carry

    lax.fori_loop(0, qi, pair_step, 0)
    scores(2 * qi + 1, s1, x1, upper)
    softmax(s0, x0, p0, a0, lower, 0)
    softmax(s0, x0, p0, a0, upper, None)
    value(jnp.maximum(qi - 1, 0), 1, p1, a1, strips)
    softmax(s1, x1, p1, a1, upper, ATT_TK)
    value(qi, 0, p0, a0, strips)
    value(qi, 1, p1, a1, upper)
    return acc_ref[:LANES, :ATT_LANES] / acc_ref[LANES:LANES + 1, :ATT_LANES]


def _mla_attn_body(q_ref, k_ref, vt_ref, o_ref, *scr):
    qi = pl.program_id(2)
    qt_ref = scr[0]
    heads = ATT_LANES // ATT_TQ
    for h in range(heads):
        qt_ref[:, h * ATT_TQ:(h + 1) * ATT_TQ] = q_ref[0, :, h * LANES:(h + 1) * LANES].astype(F32).T.astype(BF16)
    per_head = ATT_TQ // ATT_STRIP
    ot = _flash_t(k_ref, vt_ref, qt_ref, qi, scr[1:], lambda c: c // per_head, lambda c: c // (2 * per_head))
    row = lax.broadcasted_iota(jnp.int32, (LANES, ATT_TQ), 0)
    for r in range(heads // 2):
        lo = ot[:, 2 * r * ATT_TQ:(2 * r + 1) * ATT_TQ]
        hi = ot[:, (2 * r + 1) * ATT_TQ:(2 * r + 2) * ATT_TQ]
        o_ref[0, :, r * LANES:(r + 1) * LANES] = jnp.where(row < MLA_V, lo, hi).T.astype(BF16)


def _attn_specs(s, qk_width, v_width):
    n = ATT_LANES
    row = pltpu.VMEM((1, n), F32)
    return dict(
        in_specs=[pl.BlockSpec((1, ATT_TQ, qk_width), lambda b, h, i: (b, i, h)),
                  pl.BlockSpec((1, s, qk_width), lambda b, h, i: (b, 0, h)),
                  pl.BlockSpec((1, s // ATT_TQ, v_width, ATT_TQ), lambda b, h, i: (b, 0, h, 0))],
        out_specs=pl.BlockSpec((1, ATT_TQ, v_width), lambda b, h, i: (b, i, h)),
        scratch_shapes=[pltpu.VMEM((LANES, n), BF16),
                        pltpu.VMEM((ATT_TK, n + ATT_PAD), F32), pltpu.VMEM((ATT_TK, n + ATT_PAD), F32),
                        pltpu.VMEM((ATT_TK // 2, n + ATT_PAD), jnp.uint32),
                        pltpu.VMEM((ATT_TK // 2, n + ATT_PAD), jnp.uint32),
                        row, row, row, row, row, pltpu.VMEM((LANES + ATT_SUM_ROWS, n + ATT_PAD), F32)],
        compiler_params=_params(("parallel", "parallel", "arbitrary")))


def _mla_attn(q, k, vt):
    nb, s, _ = q.shape
    heads = ATT_LANES // ATT_TQ
    return pl.pallas_call(
        _mla_attn_body,
        grid=(nb, MLA_HEADS // heads, s // ATT_TQ),
        out_shape=jax.ShapeDtypeStruct((nb, s, MLA_HEADS * MLA_V), BF16),
        name="mla_attn", **_attn_specs(s, heads * LANES, heads * MLA_V),
    )(q, k, vt)


def _diff_attn_body(lam_ref, q_ref, k_ref, vt_ref, sub_ref, o_ref, *scr, out_scale):
    qi = pl.program_id(2)
    qt_ref = scr[0]
    heads = ATT_LANES // (2 * ATT_TQ)
    for g in range(heads):
        qt = q_ref[0, :, g * LANES:(g + 1) * LANES].astype(F32).T
        row = lax.broadcasted_iota(jnp.int32, qt.shape, 0)
        zero = jnp.zeros_like(qt)
        qt_ref[:, 2 * g * ATT_TQ:(2 * g + 1) * ATT_TQ] = jnp.where(row < DIFF_HEAD_DIM, qt, zero).astype(BF16)
        qt_ref[:, (2 * g + 1) * ATT_TQ:(2 * g + 2) * ATT_TQ] = jnp.where(row < DIFF_HEAD_DIM, zero, qt).astype(BF16)
    per_head = 2 * ATT_TQ // ATT_STRIP
    ot = _flash_t(k_ref, vt_ref, qt_ref, qi, scr[1:], lambda c: c // per_head, lambda c: c // per_head)
    for g in range(heads):
        o = (ot[:, 2 * g * ATT_TQ:(2 * g + 1) * ATT_TQ]
             - lam_ref[0, 0] * ot[:, (2 * g + 1) * ATT_TQ:(2 * g + 2) * ATT_TQ]).T
        o_ref[0, :, g * LANES:(g + 1) * LANES] = (_rms(o, sub_ref[...]) * out_scale).astype(BF16)


def _diff_attn(lam, q, k, vt, subln, out_scale):
    nb, s, _ = q.shape
    heads = ATT_LANES // (2 * ATT_TQ)
    spec = _attn_specs(s, heads * LANES, heads * LANES)
    spec["in_specs"] = ([pl.BlockSpec(memory_space=pltpu.SMEM)] + spec["in_specs"]
                        + [pl.BlockSpec((1, LANES), lambda b, h, i: (0, 0))])
    return pl.pallas_call(
        functools.partial(_diff_attn_body, out_scale=out_scale),
        grid=(nb, DIFF_HEADS // heads, s // ATT_TQ),
        out_shape=jax.ShapeDtypeStruct((nb, s, DIFF_HEADS * 2 * DIFF_HEAD_DIM), BF16),
        name="diff_attn", **spec,
    )(lam, q, k, vt, subln.reshape(1, -1))


def _rope_table(positions, rot_dim, theta, first_lo, period, dup=False):
    half = rot_dim // 2
    inv = theta ** (-jnp.arange(half, dtype=F32) * 2.0 / rot_dim)
    ang = inv[:, None] * positions.astype(F32).reshape(1, -1)
    cos, sin = jnp.cos(ang), jnp.sin(ang)
    rel = np.arange(LANES) % period - first_lo
    rotated = ((rel >= 0) & (rel < rot_dim))[:, None]
    freq = np.where(rotated[:, 0], rel % half, 0)
    sign = np.where(rel < half, -1.0, 1.0).astype(np.float32)[:, None]
    copy = ((rel >= rot_dim) & (rel < rot_dim + half) & dup)[:, None]
    cos_t = jnp.where(rotated, cos[freq], np.where(copy, 0.0, 1.0).astype(np.float32))
    sin_t = jnp.where(rotated, sin[freq] * sign, 0.0)
    return cos_t.T, sin_t.T


def _s5_discretize(lam_re, lam_im, log_dt, b_re, b_im):
    dt = jnp.exp(log_dt)[:, None]
    mag = jnp.exp(lam_re * dt)
    ang = lam_im * dt
    ab_re = mag * jnp.cos(ang)
    ab_im = mag * jnp.sin(ang)
    den = lam_re * lam_re + lam_im * lam_im
    nr = ab_re - 1.0
    f_re = (nr * lam_re + ab_im * lam_im) / den
    f_im = (ab_im * lam_re - nr * lam_im) / den
    bb_re = f_re[..., None] * b_re - f_im[..., None] * b_im
    bb_im = f_re[..., None] * b_im + f_im[..., None] * b_re
    return ab_re, ab_im, bb_re, bb_im


def _block_diag_in(bb):
    eye = jnp.eye(S5_GROUPS, dtype=F32)
    return jnp.einsum('gph,gk->ghkp', bb, eye).reshape(S5_WIDTH, S5_FEATS)


def _block_diag_out(c):
    eye = jnp.eye(S5_GROUPS, dtype=F32)
    return jnp.einsum('ghp,gk->gpkh', c, eye).reshape(S5_FEATS, S5_WIDTH)


def _s5_in_tiles(bb):
    full = _block_diag_in(bb)
    tiles = [full[(n * S5_TILE // S5_EXPAND // S5_TILE) * S5_TILE:, n * S5_TILE:(n + 1) * S5_TILE][:S5_TILE]
             for n in range(S5_FEATS // S5_TILE)]
    return jnp.stack(tiles).astype(BF16)


def _s5_out_tiles(c_re, c_im):
    full_re, full_im = _block_diag_out(c_re), -_block_diag_out(c_im)
    rows = S5_TILE * S5_EXPAND
    tiles = [jnp.concatenate([full_re[k * rows:(k + 1) * rows, k * S5_TILE:(k + 1) * S5_TILE],
                              full_im[k * rows:(k + 1) * rows, k * S5_TILE:(k + 1) * S5_TILE]], axis=0)
             for k in range(S5_WIDTH // S5_TILE)]
    return jnp.stack(tiles).astype(BF16)


def _pad_heads(w, heads, width, lo, hi):
    kdim = w.shape[0]
    w = w.reshape(kdim, heads, width)[:, :, lo:hi]
    w = jnp.pad(w, ((0, 0), (0, 0), (0, LANES - (hi - lo))))
    return w.reshape(kdim, heads * LANES)


def kernel(x, positions, ffn_norm, ffn_w_gu, ffn_w_down, ev_norm, ev_w_in, s5_lambda_re, s5_lambda_im, s5_log_dt, s5_b_re, s5_b_im, s5_c_re, s5_c_im, s5_d, s5_w_glu, s5_b_glu, mla_q_norm, mla_w_uq, mla_kv_norm, mla_w_ukv, ev_w_out, od_norm, od_w_in, diff_lq1, diff_lk1, diff_lq2, diff_lk2, diff_subln, od_w_out, final_norm):
    nb, s, d = x.shape
    t = nb * s
    xt = x.reshape(t, d)
    wgu = ffn_w_gu.astype(BF16)
    wdn = ffn_w_down.astype(BF16)

    xt = _ffn(xt, [], ffn_norm[0, 0], wgu, wdn, 0, 0)

    cos_m, sin_m = _rope_table(positions, MLA_ROPE, MLA_ROPE_THETA, MLA_NOPE, LANES, dup=True)
    w_in = ev_w_in[0]
    o_kr = S5_WIDTH + MLA_Q_LORA + MLA_KV_LORA
    half_m = MLA_ROPE // 2
    w_kr = jnp.concatenate([w_in[:, o_kr:], w_in[:, o_kr:o_kr + half_m]], axis=1)
    w_kr = jnp.pad(w_kr, ((0, 0), (MLA_NOPE, LANES - MLA_NOPE - MLA_ROPE - half_m)))
    win_p = jnp.concatenate([w_in[:, :o_kr], w_kr], axis=1).astype(BF16)
    wuq = mla_w_uq[0].reshape(MLA_Q_LORA, MLA_HEADS, MLA_NOPE + MLA_ROPE)
    wuq = jnp.concatenate([wuq, wuq[:, :, MLA_NOPE:MLA_NOPE + half_m]], axis=2)
    wuq_p = _pad_heads(wuq.reshape(MLA_Q_LORA, -1), MLA_HEADS, MLA_NOPE + MLA_ROPE + half_m, 0,
                       MLA_NOPE + MLA_ROPE + half_m).astype(BF16)
    wuk_p = _pad_heads(mla_w_ukv[0], MLA_HEADS, MLA_NOPE + MLA_V, 0, MLA_NOPE).astype(BF16)
    wuv = mla_w_ukv[0].reshape(MLA_KV_LORA, MLA_HEADS, MLA_NOPE + MLA_V)[:, :, MLA_NOPE:]
    wuvt = wuv.reshape(MLA_KV_LORA, MLA_HEADS * MLA_V).T.astype(BF16)
    u, q, k, vt = _even_prep(xt, ev_norm[0], win_p, mla_q_norm[0], wuq_p, mla_kv_norm[0], wuk_p, wuvt,
                             cos_m, sin_m, nb)

    ab_re, ab_im, bb_re, bb_im = _s5_discretize(s5_lambda_re[0], s5_lambda_im[0], s5_log_dt[0], s5_b_re[0], s5_b_im[0])
    s5_out = _s5(u.reshape(nb, s, S5_WIDTH), ab_re.reshape(S5_ROWS, LANES), ab_im.reshape(S5_ROWS, LANES),
                 _s5_in_tiles(bb_re), _s5_in_tiles(bb_im), _s5_out_tiles(s5_c_re[0], s5_c_im[0]),
                 s5_d[0].reshape(-1), s5_w_glu[0].astype(BF16), s5_b_glu[0])

    o = _mla_attn(q.reshape(nb, s, -1), k.reshape(nb, s, -1), vt)
    w_out = ev_w_out[0].astype(BF16)
    xt = _ffn(xt, [(s5_out.reshape(t, S5_WIDTH), w_out[:S5_WIDTH]), (o.reshape(t, -1), w_out[S5_WIDTH:])],
              ffn_norm[0, 1], wgu, wdn, 0, 1)

    xt = _ffn(xt, [], ffn_norm[1, 0], wgu, wdn, 1, 0)
    cos_t, sin_t = _rope_table(positions, DIFF_ROT, ROPE_THETA, 0, DIFF_HEAD_DIM)
    width = DIFF_HEADS * 2 * DIFF_HEAD_DIM
    w_in = od_w_in[0]
    q, k, vt = _odd_prep(xt, od_norm[0], w_in[:, :2 * width].astype(BF16), w_in[:, 2 * width:].T.astype(BF16),
                         cos_t, sin_t, nb)
    lam_init = 0.8 - 0.6 * math.exp(-0.3 * 1)
    lam = (jnp.exp(jnp.sum(diff_lq1[0] * diff_lk1[0])) - jnp.exp(jnp.sum(diff_lq2[0] * diff_lk2[0])) + lam_init)
    o = _diff_attn(lam.reshape(1, 1), q.reshape(nb, s, -1), k.reshape(nb, s, -1), vt, diff_subln[0],
                   1.0 - lam_init)
    xt = _ffn(xt, [(o.reshape(t, -1), od_w_out[0].astype(BF16))], ffn_norm[1, 1], wgu, wdn, 1, 1,
              gfinal=final_norm)
    return xt.reshape(nb, s, d)
```

```python
import functools
import math

import jax
import jax.numpy as jnp
import numpy as np
from jax import lax
from jax.experimental import pallas as pl
from jax.experimental.pallas import tpu as pltpu

F32 = jnp.float32
BF16 = jnp.bfloat16

D_MODEL = 1024
NORM_EPS = 1e-6
D_FF = 2816
S5_WIDTH = 512
S5_GROUP_CH = 16
S5_GROUPS = 32
S5_STATE = 64
S5_FEATS = S5_GROUPS * S5_STATE
MLA_HEADS = 8
MLA_NOPE = 64
MLA_ROPE = 32
MLA_V = 64
MLA_Q_LORA = 384
MLA_KV_LORA = 256
MLA_ROPE_THETA = 10000.0
DIFF_HEADS = 8
DIFF_HEAD_DIM = 64
DIFF_ROT = 16
ROPE_THETA = 500000.0

LANES = 128
SUBLANES = 8
VMEM_LIMIT = 56 * 1024 * 1024
LOG2E = math.log2(math.e)

FFN_TM = 512
FFN_CHUNKS = 1
S5_TS = 512
S5_PITCH = S5_TS + 4
S5_ROWS = S5_FEATS // LANES
S5_TILE = 256
S5_EXPAND = S5_FEATS // S5_WIDTH
ATT_TQ = 1024
ATT_TK = ATT_TQ // 2
ATT_LANES = 2 * ATT_TQ
ATT_STRIP = 256
ATT_SUM_ROWS = 16
ATT_PAD = LANES
PREP_TM = ATT_TQ

_NT = (((1,), (1,)), ((), ()))


def _rms(x, g):
    ms = jnp.mean(x * x, axis=-1, keepdims=True)
    return x * lax.rsqrt(ms + NORM_EPS) * g


def _const_spec(shape):
    nd = len(shape)
    return pl.BlockSpec(shape, lambda *_: (0,) * nd, pipeline_mode=pl.Buffered(1))


def _params(sem, flags=None):
    return pltpu.CompilerParams(dimension_semantics=sem, vmem_limit_bytes=VMEM_LIMIT, flags=flags)


def _ffn_body(*refs, n_mix, final):
    x_ref = refs[0]
    mix_refs = refs[1:1 + 2 * n_mix]
    g_ref, wgu_ref, wd_ref = refs[1 + 2 * n_mix:4 + 2 * n_mix]
    gf_ref = refs[4 + 2 * n_mix] if final else None
    o_ref = refs[-1]

    x = x_ref[...]
    for i in range(n_mix):
        x = x + jnp.dot(mix_refs[2 * i][...], mix_refs[2 * i + 1][...], preferred_element_type=F32)
    h = _rms(x, g_ref[...]).astype(BF16)
    ck = D_FF // FFN_CHUNKS
    y = None
    for c in range(FFN_CHUNKS):
        g = jnp.dot(h, wgu_ref[:, c * ck:(c + 1) * ck], preferred_element_type=F32)
        u = jnp.dot(h, wgu_ref[:, D_FF + c * ck:D_FF + (c + 1) * ck], preferred_element_type=F32)
        a = (g * jax.nn.sigmoid(g) * u).astype(BF16)
        d = jnp.dot(a, wd_ref[c * ck:(c + 1) * ck, :], preferred_element_type=F32)
        y = d if y is None else y + d
    out = x + 0.5 * y
    if final:
        out = _rms(out, gf_ref[...])
    o_ref[...] = out


def _ffn(x, mixes, g, wgu, wd, layer, pos, gfinal=None):
    t = x.shape[0]
    n_mix = len(mixes)
    final = gfinal is not None
    in_specs = [pl.BlockSpec((FFN_TM, D_MODEL), lambda i: (i, 0))]
    args = [x]
    for m, w in mixes:
        in_specs += [pl.BlockSpec((FFN_TM, m.shape[1]), lambda i: (i, 0)), _const_spec(w.shape)]
        args += [m, w]

    def slab(w):
        return pl.BlockSpec((None, None) + w.shape[2:], lambda i: (layer, pos, 0, 0), pipeline_mode=pl.Buffered(1))

    in_specs += [_const_spec((1, D_MODEL)), slab(wgu), slab(wd)]
    args += [g.reshape(1, D_MODEL), wgu, wd]
    if final:
        in_specs.append(_const_spec((1, D_MODEL)))
        args.append(gfinal.reshape(1, D_MODEL))
    return pl.pallas_call(
        functools.partial(_ffn_body, n_mix=n_mix, final=final),
        grid=(t // FFN_TM,),
        in_specs=in_specs,
        out_specs=pl.BlockSpec((FFN_TM, D_MODEL), lambda i: (i, 0)),
        out_shape=jax.ShapeDtypeStruct((t, D_MODEL), F32),
        compiler_params=_params(("parallel",)),
        name="ffn",
    )(*args)


def _rope(blk, cos, sin, half, first_lo, period):
    lane = lax.broadcasted_iota(jnp.int32, blk.shape, 1) % period
    fwd = pltpu.roll(blk, LANES - half, 1)
    bwd = pltpu.roll(blk, half, 1)
    rot = jnp.where(lane < first_lo + half, fwd, bwd)
    return blk * cos + rot * sin


def _rope_dup(blk, cos, sin, half):
    return blk * cos + pltpu.roll(blk, LANES - half, 1) * sin


def _even_prep_body(x_ref, g_ref, win_ref, qn_ref, wuq_ref, kvn_ref, wuk_ref, wuvt_ref, cos_ref, sin_ref,
                    u_ref, q_ref, k_ref, vt_ref):
    h = _rms(x_ref[...], g_ref[...]).astype(BF16)
    proj = jnp.dot(h, win_ref[...], preferred_element_type=F32)
    o_cq = S5_WIDTH
    o_ckv = o_cq + MLA_Q_LORA
    o_kr = o_ckv + MLA_KV_LORA
    u_ref[...] = proj[:, :o_cq]
    cq = proj[:, o_cq:o_ckv]
    ckv = proj[:, o_ckv:o_kr]
    kr = proj[:, o_kr:o_kr + LANES]
    cos = cos_ref[...]
    sin = sin_ref[...]
    half = MLA_ROPE // 2
    scale = (MLA_NOPE + MLA_ROPE) ** -0.5 * LOG2E
    q = jnp.dot(_rms(cq, qn_ref[...]).astype(BF16), wuq_ref[...], preferred_element_type=F32)
    ckn = _rms(ckv, kvn_ref[...]).astype(BF16)
    kn = jnp.dot(ckn, wuk_ref[...], preferred_element_type=F32)
    vt_ref[0, 0] = lax.dot_general(wuvt_ref[...], ckn, _NT, preferred_element_type=F32).astype(BF16)
    kpe = _rope_dup(kr, cos, sin, half)
    for hd in range(MLA_HEADS):
        sl = slice(hd * LANES, (hd + 1) * LANES)
        q_ref[:, sl] = (_rope_dup(q[:, sl], cos, sin, half) * scale).astype(BF16)
        k_ref[:, sl] = (kn[:, sl] + kpe).astype(BF16)


def _vt_spec(nblk, rows):
    return pl.BlockSpec((1, 1, rows, PREP_TM), lambda i: (i // nblk, i % nblk, 0, 0))


def _even_prep(x, g, win, qn, wuq, kvn, wuk, wuvt, cos, sin, nb):
    t = x.shape[0]
    tm = PREP_TM
    nblk = t // nb // tm
    row = lambda w: pl.BlockSpec((tm, w), lambda i: (i, 0))
    return pl.pallas_call(
        _even_prep_body,
        grid=(t // tm,),
        in_specs=[row(D_MODEL), _const_spec((1, D_MODEL)), _const_spec(win.shape),
                  _const_spec((1, MLA_Q_LORA)), _const_spec(wuq.shape),
                  _const_spec((1, MLA_KV_LORA)), _const_spec(wuk.shape), _const_spec(wuvt.shape),
                  row(LANES), row(LANES)],
        out_specs=[row(S5_WIDTH), row(MLA_HEADS * LANES), row(MLA_HEADS * LANES),
                   _vt_spec(nblk, MLA_HEADS * MLA_V)],
        out_shape=[jax.ShapeDtypeStruct((t, S5_WIDTH), F32),
                   jax.ShapeDtypeStruct((t, MLA_HEADS * LANES), BF16),
                   jax.ShapeDtypeStruct((t, MLA_HEADS * LANES), BF16),
                   jax.ShapeDtypeStruct((nb, nblk, MLA_HEADS * MLA_V, tm), BF16)],
        compiler_params=_params(("parallel",)),
        name="even_prep",
    )(x, g.reshape(1, -1), win, qn.reshape(1, -1), wuq, kvn.reshape(1, -1), wuk, wuvt, cos, sin)


def _odd_prep_body(x_ref, g_ref, wqk_ref, wvt_ref, cos_ref, sin_ref, q_ref, k_ref, vt_ref):
    h = _rms(x_ref[...], g_ref[...]).astype(BF16)
    cos = cos_ref[...]
    sin = sin_ref[...]
    half = DIFF_ROT // 2
    width = DIFF_HEADS * 2 * DIFF_HEAD_DIM
    scale = DIFF_HEAD_DIM ** -0.5 * LOG2E
    q = jnp.dot(h, wqk_ref[:, :width], preferred_element_type=F32)
    k = jnp.dot(h, wqk_ref[:, width:], preferred_element_type=F32)
    vt_ref[0, 0] = lax.dot_general(wvt_ref[...], h, _NT, preferred_element_type=F32).astype(BF16)
    for c in range(width // LANES):
        sl = slice(c * LANES, (c + 1) * LANES)
        q_ref[:, sl] = (_rope(q[:, sl], cos, sin, half, 0, DIFF_HEAD_DIM) * scale).astype(BF16)
        k_ref[:, sl] = _rope(k[:, sl], cos, sin, half, 0, DIFF_HEAD_DIM).astype(BF16)


def _odd_prep(x, g, wqk, wvt, cos, sin, nb):
    t = x.shape[0]
    tm = PREP_TM
    nblk = t // nb // tm
    width = DIFF_HEADS * 2 * DIFF_HEAD_DIM
    row = lambda w: pl.BlockSpec((tm, w), lambda i: (i, 0))
    return pl.pallas_call(
        _odd_prep_body,
        grid=(t // tm,),
        in_specs=[row(D_MODEL), _const_spec((1, D_MODEL)), _const_spec(wqk.shape), _const_spec(wvt.shape),
                  row(LANES), row(LANES)],
        out_specs=[row(width), row(width), _vt_spec(nblk, width)],
        out_shape=[jax.ShapeDtypeStruct((t, width), BF16), jax.ShapeDtypeStruct((t, width), BF16),
                   jax.ShapeDtypeStruct((nb, nblk, width, tm), BF16)],
        compiler_params=_params(("parallel",)),
        name="odd_prep",
    )(x, g.reshape(1, -1), wqk, wvt, cos, sin)


def _s5_body(u_ref, ar_ref, ai_ref, bre_ref, bim_ref, c_ref, d_ref, wglu_ref, bglu_ref, o_ref,
             xr_ref, xi_ref, sr_ref, si_ref):
    nb = u_ref.shape[0]

    @pl.when(pl.program_id(0) == 0)
    def _():
        sr_ref[...] = jnp.zeros_like(sr_ref)
        si_ref[...] = jnp.zeros_like(si_ref)

    for b in range(nb):
        ub = u_ref[b].astype(BF16)
        for n in range(S5_FEATS // S5_TILE):
            k = n * S5_TILE // S5_EXPAND // S5_TILE
            uk = ub[:, k * S5_TILE:(k + 1) * S5_TILE]
            bur = jnp.dot(uk, bre_ref[n], preferred_element_type=F32)
            bui = jnp.dot(uk, bim_ref[n], preferred_element_type=F32)
            for h in range(S5_TILE // LANES):
                j = n * (S5_TILE // LANES) + h
                xr_ref[b, j * S5_PITCH:j * S5_PITCH + S5_TS, :] = bur[:, h * LANES:(h + 1) * LANES]
                xi_ref[b, j * S5_PITCH:j * S5_PITCH + S5_TS, :] = bui[:, h * LANES:(h + 1) * LANES]

    ar = ar_ref[...]
    ai = ai_ref[...]

    def step(t, carry):
        new = []
        for b in range(nb):
            pr, pi = carry[2 * b], carry[2 * b + 1]
            idx = pl.ds(t, S5_ROWS, stride=S5_PITCH)
            nr = ar * pr - ai * pi + xr_ref[b, idx, :]
            ni = ar * pi + ai * pr + xi_ref[b, idx, :]
            xr_ref[b, idx, :] = nr
            xi_ref[b, idx, :] = ni
            new += [nr, ni]
        return tuple(new)

    init = []
    for b in range(nb):
        init += [sr_ref[b], si_ref[b]]
    fin = lax.fori_loop(0, S5_TS, step, tuple(init), unroll=8)
    for b in range(nb):
        sr_ref[b] = fin[2 * b]
        si_ref[b] = fin[2 * b + 1]

    per_tile = S5_ROWS * S5_TILE // S5_WIDTH
    for b in range(nb):
        ys = []
        for k in range(S5_WIDTH // S5_TILE):
            slabs = range(k * per_tile, (k + 1) * per_tile)
            parts = [xr_ref[b, j * S5_PITCH:j * S5_PITCH + S5_TS, :].astype(BF16) for j in slabs]
            parts += [xi_ref[b, j * S5_PITCH:j * S5_PITCH + S5_TS, :].astype(BF16) for j in slabs]
            ys.append(jnp.dot(jnp.concatenate(parts, axis=1), c_ref[k], preferred_element_type=F32))
        y = jnp.concatenate(ys, axis=1) + d_ref[...] * u_ref[b]
        g = 0.5 * y * (1.0 + jnp.tanh(math.sqrt(2.0 / math.pi) * (y + 0.044715 * (y * y * y))))
        z = jnp.dot(g.astype(BF16), wglu_ref[...], preferred_element_type=F32) + bglu_ref[...]
        o_ref[b] = (g * jax.nn.sigmoid(z)).astype(BF16)


def _s5(u, ar, ai, bre, bim, cmat, dskip, wglu, bglu):
    nb, s, _ = u.shape
    blk = pl.BlockSpec((nb, S5_TS, S5_WIDTH), lambda i: (0, i, 0))
    return pl.pallas_call(
        _s5_body,
        grid=(s // S5_TS,),
        in_specs=[blk, _const_spec(ar.shape), _const_spec(ai.shape), _const_spec(bre.shape),
                  _const_spec(bim.shape), _const_spec(cmat.shape), _const_spec((1, S5_WIDTH)),
                  _const_spec(wglu.shape), _const_spec((1, S5_WIDTH))],
        out_specs=blk,
        out_shape=jax.ShapeDtypeStruct((nb, s, S5_WIDTH), BF16),
        scratch_shapes=[pltpu.VMEM((nb, S5_ROWS * S5_PITCH, LANES), F32),
                        pltpu.VMEM((nb, S5_ROWS * S5_PITCH, LANES), F32),
                        pltpu.VMEM((nb, S5_ROWS, LANES), F32),
                        pltpu.VMEM((nb, S5_ROWS, LANES), F32)],
        compiler_params=_params(("arbitrary",)),
        name="s5",
    )(u, ar, ai, bre, bim, cmat, dskip.reshape(1, -1), wglu, bglu.reshape(1, -1))


def _flash_t(k_ref, vt_ref, qt_ref, qi, scr, k_block, v_block):
    s0, s1, p0, p1, a0, a1, x0, x1, m_ref, acc_ref = scr
    strips = tuple(range(ATT_LANES // ATT_STRIP))

    def q_off(c):
        return (c * ATT_STRIP) % ATT_TQ

    lower = tuple(c for c in strips if q_off(c) < ATT_TK)
    upper = tuple(c for c in strips if q_off(c) >= ATT_TK)
    m_ref[...] = jnp.full(m_ref.shape, -jnp.inf, F32)
    acc_ref[...] = jnp.zeros(acc_ref.shape, F32)
    ones = jnp.ones((ATT_SUM_ROWS, ATT_TK), BF16)

    def scores(j, s_ref, x_ref, which):
        for c in which:
            sl = slice(c * ATT_STRIP, (c + 1) * ATT_STRIP)
            kb = k_ref[0, pl.ds(j * ATT_TK, ATT_TK), k_block(c) * LANES:(k_block(c) + 1) * LANES]
            s = jnp.dot(kb, qt_ref[:, sl], preferred_element_type=F32)
            s_ref[:, sl] = s
            x_ref[:, sl] = jnp.max(s, axis=0, keepdims=True)

    def softmax(s_ref, x_ref, p_ref, a_ref, which, key_off):
        for c in which:
            for t in range(ATT_STRIP // LANES):
                sl = slice(c * ATT_STRIP + t * LANES, c * ATT_STRIP + (t + 1) * LANES)
                s = s_ref[:, sl]
                if key_off is None:
                    s_max = x_ref[:, sl]
                else:
                    kpos = lax.broadcasted_iota(jnp.int32, s.shape, 0) + key_off
                    qpos = lax.broadcasted_iota(jnp.int32, s.shape, 1) + (q_off(c) + t * LANES)
                    s = jnp.where(kpos <= qpos, s, -jnp.inf)
                    s_max = jnp.max(s, axis=0, keepdims=True)
                m_old = m_ref[:, sl]
                m_new = jnp.maximum(m_old, s_max)
                a_ref[:, sl] = jnp.exp2(m_old - m_new)
                m_ref[:, sl] = m_new
                p_ref[:, sl] = pltpu.bitcast(jnp.exp2(s - m_new).astype(BF16), jnp.uint32)

    def value(pair, half, p_ref, a_ref, which):
        for c in which:
            sl = slice(c * ATT_STRIP, (c + 1) * ATT_STRIP)
            vt = vt_ref[0, pair, v_block(c) * LANES:(v_block(c) + 1) * LANES, half * ATT_TK:(half + 1) * ATT_TK]
            p = pltpu.bitcast(p_ref[:, sl], BF16)
            pv = jnp.dot(jnp.concatenate([vt, ones], axis=0), p, preferred_element_type=F32)
            acc_ref[:, sl] = a_ref[:, sl] * acc_ref[:, sl] + pv

    scores(0, s0, x0, strips)

    def pair_step(i, carry):
        softmax(s0, x0, p0, a0, strips, None)
        scores(2 * i + 1, s1, x1, strips)
        value(i, 0, p0, a0, strips)
        softmax(s1, x1, p1, a1, strips, None)
        scores(2 * i + 2, s0, x0, strips)
        value(i, 1, p1, a1, strips)
        return carry

    lax.fori_loop(0, qi, pair_step, 0)
    scores(2 * qi + 1, s1, x1, upper)
    softmax(s0, x0, p0, a0, lower, 0)
    softmax(s0, x0, p0, a0, upper, None)
    value(qi, 0, p0, a0, strips)
    softmax(s1, x1, p1, a1, upper, ATT_TK)
    value(qi, 1, p1, a1, upper)
    return acc_ref[:LANES, :ATT_LANES] / acc_ref[LANES:LANES + 1, :ATT_LANES]


def _mla_attn_body(q_ref, k_ref, vt_ref, o_ref, *scr):
    qi = pl.program_id(2)
    qt_ref = scr[0]
    heads = ATT_LANES // ATT_TQ
    for h in range(heads):
        qt_ref[:, h * ATT_TQ:(h + 1) * ATT_TQ] = q_ref[0, :, h * LANES:(h + 1) * LANES].astype(F32).T.astype(BF16)
    per_head = ATT_TQ // ATT_STRIP
    ot = _flash_t(k_ref, vt_ref, qt_ref, qi, scr[1:], lambda c: c // per_head, lambda c: c // (2 * per_head))
    row = lax.broadcasted_iota(jnp.int32, (LANES, ATT_TQ), 0)
    for r in range(heads // 2):
        lo = ot[:, 2 * r * ATT_TQ:(2 * r + 1) * ATT_TQ]
        hi = ot[:, (2 * r + 1) * ATT_TQ:(2 * r + 2) * ATT_TQ]
        o_ref[0, :, r * LANES:(r + 1) * LANES] = jnp.where(row < MLA_V, lo, hi).T.astype(BF16)


def _attn_specs(s, qk_width, v_width):
    n = ATT_LANES
    row = pltpu.VMEM((1, n), F32)
    return dict(
        in_specs=[pl.BlockSpec((1, ATT_TQ, qk_width), lambda b, h, i: (b, i, h)),
                  pl.BlockSpec((1, s, qk_width), lambda b, h, i: (b, 0, h)),
                  pl.BlockSpec((1, s // ATT_TQ, v_width, ATT_TQ), lambda b, h, i: (b, 0, h, 0))],
        out_specs=pl.BlockSpec((1, ATT_TQ, v_width), lambda b, h, i: (b, i, h)),
        scratch_shapes=[pltpu.VMEM((LANES, n), BF16),
                        pltpu.VMEM((ATT_TK, n + ATT_PAD), F32), pltpu.VMEM((ATT_TK, n + ATT_PAD), F32),
                        pltpu.VMEM((ATT_TK // 2, n + ATT_PAD), jnp.uint32),
                        pltpu.VMEM((ATT_TK // 2, n + ATT_PAD), jnp.uint32),
                        row, row, row, row, row, pltpu.VMEM((LANES + ATT_SUM_ROWS, n + ATT_PAD), F32)],
        compiler_params=_params(("parallel", "parallel", "arbitrary")))


def _mla_attn(q, k, vt):
    nb, s, _ = q.shape
    heads = ATT_LANES // ATT_TQ
    return pl.pallas_call(
        _mla_attn_body,
        grid=(nb, MLA_HEADS // heads, s // ATT_TQ),
        out_shape=jax.ShapeDtypeStruct((nb, s, MLA_HEADS * MLA_V), BF16),
        name="mla_attn", **_attn_specs(s, heads * LANES, heads * MLA_V),
    )(q, k, vt)


def _diff_attn_body(lam_ref, q_ref, k_ref, vt_ref, sub_ref, o_ref, *scr, out_scale):
    qi = pl.program_id(2)
    qt_ref = scr[0]
    heads = ATT_LANES // (2 * ATT_TQ)
    for g in range(heads):
        qt = q_ref[0, :, g * LANES:(g + 1) * LANES].astype(F32).T
        row = lax.broadcasted_iota(jnp.int32, qt.shape, 0)
        zero = jnp.zeros_like(qt)
        qt_ref[:, 2 * g * ATT_TQ:(2 * g + 1) * ATT_TQ] = jnp.where(row < DIFF_HEAD_DIM, qt, zero).astype(BF16)
        qt_ref[:, (2 * g + 1) * ATT_TQ:(2 * g + 2) * ATT_TQ] = jnp.where(row < DIFF_HEAD_DIM, zero, qt).astype(BF16)
    per_head = 2 * ATT_TQ // ATT_STRIP
    ot = _flash_t(k_ref, vt_ref, qt_ref, qi, scr[1:], lambda c: c // per_head, lambda c: c // per_head)
    for g in range(heads):
        o = (ot[:, 2 * g * ATT_TQ:(2 * g + 1) * ATT_TQ]
             - lam_ref[0, 0] * ot[:, (2 * g + 1) * ATT_TQ:(2 * g + 2) * ATT_TQ]).T
        o_ref[0, :, g * LANES:(g + 1) * LANES] = (_rms(o, sub_ref[...]) * out_scale).astype(BF16)


def _diff_attn(lam, q, k, vt, subln, out_scale):
    nb, s, _ = q.shape
    heads = ATT_LANES // (2 * ATT_TQ)
    spec = _attn_specs(s, heads * LANES, heads * LANES)
    spec["in_specs"] = ([pl.BlockSpec(memory_space=pltpu.SMEM)] + spec["in_specs"]
                        + [pl.BlockSpec((1, LANES), lambda b, h, i: (0, 0))])
    return pl.pallas_call(
        functools.partial(_diff_attn_body, out_scale=out_scale),
        grid=(nb, DIFF_HEADS // heads, s // ATT_TQ),
        out_shape=jax.ShapeDtypeStruct((nb, s, DIFF_HEADS * 2 * DIFF_HEAD_DIM), BF16),
        name="diff_attn", **spec,
    )(lam, q, k, vt, subln.reshape(1, -1))


def _rope_table(positions, rot_dim, theta, first_lo, period, dup=False):
    half = rot_dim // 2
    inv = theta ** (-jnp.arange(half, dtype=F32) * 2.0 / rot_dim)
    ang = inv[:, None] * positions.astype(F32).reshape(1, -1)
    cos, sin = jnp.cos(ang), jnp.sin(ang)
    rel = np.arange(LANES) % period - first_lo
    rotated = ((rel >= 0) & (rel < rot_dim))[:, None]
    freq = np.where(rotated[:, 0], rel % half, 0)
    sign = np.where(rel < half, -1.0, 1.0).astype(np.float32)[:, None]
    copy = ((rel >= rot_dim) & (rel < rot_dim + half) & dup)[:, None]
    cos_t = jnp.where(rotated, cos[freq], np.where(copy, 0.0, 1.0).astype(np.float32))
    sin_t = jnp.where(rotated, sin[freq] * sign, 0.0)
    return cos_t.T, sin_t.T


def _s5_discretize(lam_re, lam_im, log_dt, b_re, b_im):
    dt = jnp.exp(log_dt)[:, None]
    mag = jnp.exp(lam_re * dt)
    ang = lam_im * dt
    ab_re = mag * jnp.cos(ang)
    ab_im = mag * jnp.sin(ang)
    den = lam_re * lam_re + lam_im * lam_im
    nr = ab_re - 1.0
    f_re = (nr * lam_re + ab_im * lam_im) / den
    f_im = (ab_im * lam_re - nr * lam_im) / den
    bb_re = f_re[..., None] * b_re - f_im[..., None] * b_im
    bb_im = f_re[..., None] * b_im + f_im[..., None] * b_re
    return ab_re, ab_im, bb_re, bb_im


def _block_diag_in(bb):
    eye = jnp.eye(S5_GROUPS, dtype=F32)
    return jnp.einsum('gph,gk->ghkp', bb, eye).reshape(S5_WIDTH, S5_FEATS)


def _block_diag_out(c):
    eye = jnp.eye(S5_GROUPS, dtype=F32)
    return jnp.einsum('ghp,gk->gpkh', c, eye).reshape(S5_FEATS, S5_WIDTH)


def _s5_in_tiles(bb):
    full = _block_diag_in(bb)
    tiles = [full[(n * S5_TILE // S5_EXPAND // S5_TILE) * S5_TILE:, n * S5_TILE:(n + 1) * S5_TILE][:S5_TILE]
             for n in range(S5_FEATS // S5_TILE)]
    return jnp.stack(tiles).astype(BF16)


def _s5_out_tiles(c_re, c_im):
    full_re, full_im = _block_diag_out(c_re), -_block_diag_out(c_im)
    rows = S5_TILE * S5_EXPAND
    tiles = [jnp.concatenate([full_re[k * rows:(k + 1) * rows, k * S5_TILE:(k + 1) * S5_TILE],
                              full_im[k * rows:(k + 1) * rows, k * S5_TILE:(k + 1) * S5_TILE]], axis=0)
             for k in range(S5_WIDTH // S5_TILE)]
    return jnp.stack(tiles).astype(BF16)


def _pad_heads(w, heads, width, lo, hi):
    kdim = w.shape[0]
    w = w.reshape(kdim, heads, width)[:, :, lo:hi]
    w = jnp.pad(w, ((0, 0), (0, 0), (0, LANES - (hi - lo))))
    return w.reshape(kdim, heads * LANES)


def kernel(x, positions, ffn_norm, ffn_w_gu, ffn_w_down, ev_norm, ev_w_in, s5_lambda_re, s5_lambda_im, s5_log_dt, s5_b_re, s5_b_im, s5_c_re, s5_c_im, s5_d, s5_w_glu, s5_b_glu, mla_q_norm, mla_w_uq, mla_kv_norm, mla_w_ukv, ev_w_out, od_norm, od_w_in, diff_lq1, diff_lk1, diff_lq2, diff_lk2, diff_subln, od_w_out, final_norm):
    nb, s, d = x.shape
    t = nb * s
    xt = x.reshape(t, d)
    wgu = ffn_w_gu.astype(BF16)
    wdn = ffn_w_down.astype(BF16)

    xt = _ffn(xt, [], ffn_norm[0, 0], wgu, wdn, 0, 0)

    cos_m, sin_m = _rope_table(positions, MLA_ROPE, MLA_ROPE_THETA, MLA_NOPE, LANES, dup=True)
    w_in = ev_w_in[0]
    o_kr = S5_WIDTH + MLA_Q_LORA + MLA_KV_LORA
    half_m = MLA_ROPE // 2
    w_kr = jnp.concatenate([w_in[:, o_kr:], w_in[:, o_kr:o_kr + half_m]], axis=1)
    w_kr = jnp.pad(w_kr, ((0, 0), (MLA_NOPE, LANES - MLA_NOPE - MLA_ROPE - half_m)))
    win_p = jnp.concatenate([w_in[:, :o_kr], w_kr], axis=1).astype(BF16)
    wuq = mla_w_uq[0].reshape(MLA_Q_LORA, MLA_HEADS, MLA_NOPE + MLA_ROPE)
    wuq = jnp.concatenate([wuq, wuq[:, :, MLA_NOPE:MLA_NOPE + half_m]], axis=2)
    wuq_p = _pad_heads(wuq.reshape(MLA_Q_LORA, -1), MLA_HEADS, MLA_NOPE + MLA_ROPE + half_m, 0,
                       MLA_NOPE + MLA_ROPE + half_m).astype(BF16)
    wuk_p = _pad_heads(mla_w_ukv[0], MLA_HEADS, MLA_NOPE + MLA_V, 0, MLA_NOPE).astype(BF16)
    wuv = mla_w_ukv[0].reshape(MLA_KV_LORA, MLA_HEADS, MLA_NOPE + MLA_V)[:, :, MLA_NOPE:]
    wuvt = wuv.reshape(MLA_KV_LORA, MLA_HEADS * MLA_V).T.astype(BF16)
    u, q, k, vt = _even_prep(xt, ev_norm[0], win_p, mla_q_norm[0], wuq_p, mla_kv_norm[0], wuk_p, wuvt,
                             cos_m, sin_m, nb)

    ab_re, ab_im, bb_re, bb_im = _s5_discretize(s5_lambda_re[0], s5_lambda_im[0], s5_log_dt[0], s5_b_re[0], s5_b_im[0])
    s5_out = _s5(u.reshape(nb, s, S5_WIDTH), ab_re.reshape(S5_ROWS, LANES), ab_im.reshape(S5_ROWS, LANES),
                 _s5_in_tiles(bb_re), _s5_in_tiles(bb_im), _s5_out_tiles(s5_c_re[0], s5_c_im[0]),
                 s5_d[0].reshape(-1), s5_w_glu[0].astype(BF16), s5_b_glu[0])

    o = _mla_attn(q.reshape(nb, s, -1), k.reshape(nb, s, -1), vt)
    w_out = ev_w_out[0].astype(BF16)
    xt = _ffn(xt, [(s5_out.reshape(t, S5_WIDTH), w_out[:S5_WIDTH]), (o.reshape(t, -1), w_out[S5_WIDTH:])],
              ffn_norm[0, 1], wgu, wdn, 0, 1)

    xt = _ffn(xt, [], ffn_norm[1, 0], wgu, wdn, 1, 0)
    cos_t, sin_t = _rope_table(positions, DIFF_ROT, ROPE_THETA, 0, DIFF_HEAD_DIM)
    width = DIFF_HEADS * 2 * DIFF_HEAD_DIM
    w_in = od_w_in[0]
    q, k, vt = _odd_prep(xt, od_norm[0], w_in[:, :2 * width].astype(BF16), w_in[:, 2 * width:].T.astype(BF16),
                         cos_t, sin_t, nb)
    lam_init = 0.8 - 0.6 * math.exp(-0.3 * 1)
    lam = (jnp.exp(jnp.sum(diff_lq1[0] * diff_lk1[0])) - jnp.exp(jnp.sum(diff_lq2[0] * diff_lk2[0])) + lam_init)
    o = _diff_attn(lam.reshape(1, 1), q.reshape(nb, s, -1), k.reshape(nb, s, -1), vt, diff_subln[0],
                   1.0 - lam_init)
    xt = _ffn(xt, [(o.reshape(t, -1), od_w_out[0].astype(BF16))], ffn_norm[1, 1], wgu, wdn, 1, 1,
              gfinal=final_norm)
    return xt.reshape(nb, s, d)
```

```python
import functools
import math

import jax
import jax.numpy as jnp
import numpy as np
from jax import lax
from jax.experimental import pallas as pl
from jax.experimental.pallas import tpu as pltpu

F32 = jnp.float32
BF16 = jnp.bfloat16

D_MODEL = 1024
NORM_EPS = 1e-6
D_FF = 2816
S5_WIDTH = 512
S5_GROUP_CH = 16
S5_GROUPS = 32
S5_STATE = 64
S5_FEATS = S5_GROUPS * S5_STATE
MLA_HEADS = 8
MLA_NOPE = 64
MLA_ROPE = 32
MLA_V = 64
MLA_Q_LORA = 384
MLA_KV_LORA = 256
MLA_ROPE_THETA = 10000.0
DIFF_HEADS = 8
DIFF_HEAD_DIM = 64
DIFF_ROT = 16
ROPE_THETA = 500000.0

LANES = 128
SUBLANES = 8
VMEM_LIMIT = 56 * 1024 * 1024
LOG2E = math.log2(math.e)

FFN_TM = 512
FFN_CHUNKS = 1
S5_TS = 512
S5_PITCH = S5_TS + 4
S5_ROWS = S5_FEATS // LANES
S5_TILE = 256
S5_EXPAND = S5_FEATS // S5_WIDTH
ATT_TQ = 1024
ATT_TK = ATT_TQ // 2
ATT_LANES = 2 * ATT_TQ
ATT_STRIP = 256
ATT_SUM_ROWS = 16
ATT_PAD = LANES
PREP_TM = ATT_TQ

_NT = (((1,), (1,)), ((), ()))


def _rms(x, g):
    ms = jnp.mean(x * x, axis=-1, keepdims=True)
    return x * lax.rsqrt(ms + NORM_EPS) * g


def _const_spec(shape):
    nd = len(shape)
    return pl.BlockSpec(shape, lambda *_: (0,) * nd, pipeline_mode=pl.Buffered(1))


def _params(sem, flags=None):
    return pltpu.CompilerParams(dimension_semantics=sem, vmem_limit_bytes=VMEM_LIMIT, flags=flags)


def _ffn_body(*refs, n_mix, final):
    x_ref = refs[0]
    mix_refs = refs[1:1 + 2 * n_mix]
    g_ref, wgu_ref, wd_ref = refs[1 + 2 * n_mix:4 + 2 * n_mix]
    gf_ref = refs[4 + 2 * n_mix] if final else None
    o_ref = refs[-1]

    x = x_ref[...]
    for i in range(n_mix):
        x = x + jnp.dot(mix_refs[2 * i][...], mix_refs[2 * i + 1][...], preferred_element_type=F32)
    h = _rms(x, g_ref[...]).astype(BF16)
    ck = D_FF // FFN_CHUNKS
    y = None
    for c in range(FFN_CHUNKS):
        g = jnp.dot(h, wgu_ref[:, c * ck:(c + 1) * ck], preferred_element_type=F32)
        u = jnp.dot(h, wgu_ref[:, D_FF + c * ck:D_FF + (c + 1) * ck], preferred_element_type=F32)
        a = (g * jax.nn.sigmoid(g) * u).astype(BF16)
        d = jnp.dot(a, wd_ref[c * ck:(c + 1) * ck, :], preferred_element_type=F32)
        y = d if y is None else y + d
    out = x + 0.5 * y
    if final:
        out = _rms(out, gf_ref[...])
    o_ref[...] = out


def _ffn(x, mixes, g, wgu, wd, layer, pos, gfinal=None):
    t = x.shape[0]
    n_mix = len(mixes)
    final = gfinal is not None
    in_specs = [pl.BlockSpec((FFN_TM, D_MODEL), lambda i: (i, 0))]
    args = [x]
    for m, w in mixes:
        in_specs += [pl.BlockSpec((FFN_TM, m.shape[1]), lambda i: (i, 0)), _const_spec(w.shape)]
        args += [m, w]

    def slab(w):
        return pl.BlockSpec((None, None) + w.shape[2:], lambda i: (layer, pos, 0, 0), pipeline_mode=pl.Buffered(1))

    in_specs += [_const_spec((1, D_MODEL)), slab(wgu), slab(wd)]
    args += [g.reshape(1, D_MODEL), wgu, wd]
    if final:
        in_specs.append(_const_spec((1, D_MODEL)))
        args.append(gfinal.reshape(1, D_MODEL))
    return pl.pallas_call(
        functools.partial(_ffn_body, n_mix=n_mix, final=final),
        grid=(t // FFN_TM,),
        in_specs=in_specs,
        out_specs=pl.BlockSpec((FFN_TM, D_MODEL), lambda i: (i, 0)),
        out_shape=jax.ShapeDtypeStruct((t, D_MODEL), F32),
        compiler_params=_params(("parallel",)),
        name="ffn",
    )(*args)


def _rope(blk, cos, sin, half, first_lo, period):
    lane = lax.broadcasted_iota(jnp.int32, blk.shape, 1) % period
    fwd = pltpu.roll(blk, LANES - half, 1)
    bwd = pltpu.roll(blk, half, 1)
    rot = jnp.where(lane < first_lo + half, fwd, bwd)
    return blk * cos + rot * sin


def _rope_dup(blk, cos, sin, half):
    return blk * cos + pltpu.roll(blk, LANES - half, 1) * sin


def _even_prep_body(x_ref, g_ref, win_ref, qn_ref, wuq_ref, kvn_ref, wuk_ref, wuvt_ref, cos_ref, sin_ref,
                    u_ref, q_ref, k_ref, vt_ref):
    h = _rms(x_ref[...], g_ref[...]).astype(BF16)
    proj = jnp.dot(h, win_ref[...], preferred_element_type=F32)
    o_cq = S5_WIDTH
    o_ckv = o_cq + MLA_Q_LORA
    o_kr = o_ckv + MLA_KV_LORA
    u_ref[...] = proj[:, :o_cq]
    cq = proj[:, o_cq:o_ckv]
    ckv = proj[:, o_ckv:o_kr]
    kr = proj[:, o_kr:o_kr + LANES]
    cos = cos_ref[...]
    sin = sin_ref[...]
    half = MLA_ROPE // 2
    scale = (MLA_NOPE + MLA_ROPE) ** -0.5 * LOG2E
    q = jnp.dot(_rms(cq, qn_ref[...]).astype(BF16), wuq_ref[...], preferred_element_type=F32)
    ckn = _rms(ckv, kvn_ref[...]).astype(BF16)
    kn = jnp.dot(ckn, wuk_ref[...], preferred_element_type=F32)
    vt_ref[0, 0] = lax.dot_general(wuvt_ref[...], ckn, _NT, preferred_element_type=F32).astype(BF16)
    kpe = _rope_dup(kr, cos, sin, half)
    for hd in range(MLA_HEADS):
        sl = slice(hd * LANES, (hd + 1) * LANES)
        q_ref[:, sl] = (_rope_dup(q[:, sl], cos, sin, half) * scale).astype(BF16)
        k_ref[:, sl] = (kn[:, sl] + kpe).astype(BF16)


def _vt_spec(nblk, rows):
    return pl.BlockSpec((1, 1, rows, PREP_TM), lambda i: (i // nblk, i % nblk, 0, 0))


def _even_prep(x, g, win, qn, wuq, kvn, wuk, wuvt, cos, sin, nb):
    t = x.shape[0]
    tm = PREP_TM
    nblk = t // nb // tm
    row = lambda w: pl.BlockSpec((tm, w), lambda i: (i, 0))
    return pl.pallas_call(
        _even_prep_body,
        grid=(t // tm,),
        in_specs=[row(D_MODEL), _const_spec((1, D_MODEL)), _const_spec(win.shape),
                  _const_spec((1, MLA_Q_LORA)), _const_spec(wuq.shape),
                  _const_spec((1, MLA_KV_LORA)), _const_spec(wuk.shape), _const_spec(wuvt.shape),
                  row(LANES), row(LANES)],
        out_specs=[row(S5_WIDTH), row(MLA_HEADS * LANES), row(MLA_HEADS * LANES),
                   _vt_spec(nblk, MLA_HEADS * MLA_V)],
        out_shape=[jax.ShapeDtypeStruct((t, S5_WIDTH), F32),
                   jax.ShapeDtypeStruct((t, MLA_HEADS * LANES), BF16),
                   jax.ShapeDtypeStruct((t, MLA_HEADS * LANES), BF16),
                   jax.ShapeDtypeStruct((nb, nblk, MLA_HEADS * MLA_V, tm), BF16)],
        compiler_params=_params(("parallel",)),
        name="even_prep",
    )(x, g.reshape(1, -1), win, qn.reshape(1, -1), wuq, kvn.reshape(1, -1), wuk, wuvt, cos, sin)


def _odd_prep_body(x_ref, g_ref, wqk_ref, wvt_ref, cos_ref, sin_ref, q_ref, k_ref, vt_ref):
    h = _rms(x_ref[...], g_ref[...]).astype(BF16)
    cos = cos_ref[...]
    sin = sin_ref[...]
    half = DIFF_ROT // 2
    width = DIFF_HEADS * 2 * DIFF_HEAD_DIM
    scale = DIFF_HEAD_DIM ** -0.5 * LOG2E
    q = jnp.dot(h, wqk_ref[:, :width], preferred_element_type=F32)
    k = jnp.dot(h, wqk_ref[:, width:], preferred_element_type=F32)
    vt_ref[0, 0] = lax.dot_general(wvt_ref[...], h, _NT, preferred_element_type=F32).astype(BF16)
    for c in range(width // LANES):
        sl = slice(c * LANES, (c + 1) * LANES)
        q_ref[:, sl] = (_rope(q[:, sl], cos, sin, half, 0, DIFF_HEAD_DIM) * scale).astype(BF16)
        k_ref[:, sl] = _rope(k[:, sl], cos, sin, half, 0, DIFF_HEAD_DIM).astype(BF16)


def _odd_prep(x, g, wqk, wvt, cos, sin, nb):
    t = x.shape[0]
    tm = PREP_TM
    nblk = t // nb // tm
    width = DIFF_HEADS * 2 * DIFF_HEAD_DIM
    row = lambda w: pl.BlockSpec((tm, w), lambda i: (i, 0))
    return pl.pallas_call(
        _odd_prep_body,
        grid=(t // tm,),
        in_specs=[row(D_MODEL), _const_spec((1, D_MODEL)), _const_spec(wqk.shape), _const_spec(wvt.shape),
                  row(LANES), row(LANES)],
        out_specs=[row(width), row(width), _vt_spec(nblk, width)],
        out_shape=[jax.ShapeDtypeStruct((t, width), BF16), jax.ShapeDtypeStruct((t, width), BF16),
                   jax.ShapeDtypeStruct((nb, nblk, width, tm), BF16)],
        compiler_params=_params(("parallel",)),
        name="odd_prep",
    )(x, g.reshape(1, -1), wqk, wvt, cos, sin)


def _s5_body(u_ref, ar_ref, ai_ref, bre_ref, bim_ref, c_ref, d_ref, wglu_ref, bglu_ref, o_ref,
             xr_ref, xi_ref, sr_ref, si_ref):
    nb = u_ref.shape[0]

    @pl.when(pl.program_id(0) == 0)
    def _():
        sr_ref[...] = jnp.zeros_like(sr_ref)
        si_ref[...] = jnp.zeros_like(si_ref)

    for b in range(nb):
        ub = u_ref[b].astype(BF16)
        for n in range(S5_FEATS // S5_TILE):
            k = n * S5_TILE // S5_EXPAND // S5_TILE
            uk = ub[:, k * S5_TILE:(k + 1) * S5_TILE]
            bur = jnp.dot(uk, bre_ref[n], preferred_element_type=F32)
            bui = jnp.dot(uk, bim_ref[n], preferred_element_type=F32)
            for h in range(S5_TILE // LANES):
                j = n * (S5_TILE // LANES) + h
                xr_ref[b, j * S5_PITCH:j * S5_PITCH + S5_TS, :] = bur[:, h * LANES:(h + 1) * LANES]
                xi_ref[b, j * S5_PITCH:j * S5_PITCH + S5_TS, :] = bui[:, h * LANES:(h + 1) * LANES]

    ar = ar_ref[...]
    ai = ai_ref[...]

    def step(t, carry):
        new = []
        for b in range(nb):
            pr, pi = carry[2 * b], carry[2 * b + 1]
            idx = pl.ds(t, S5_ROWS, stride=S5_PITCH)
            nr = ar * pr - ai * pi + xr_ref[b, idx, :]
            ni = ar * pi + ai * pr + xi_ref[b, idx, :]
            xr_ref[b, idx, :] = nr
            xi_ref[b, idx, :] = ni
            new += [nr, ni]
        return tuple(new)

    init = []
    for b in range(nb):
        init += [sr_ref[b], si_ref[b]]
    fin = lax.fori_loop(0, S5_TS, step, tuple(init), unroll=8)
    for b in range(nb):
        sr_ref[b] = fin[2 * b]
        si_ref[b] = fin[2 * b + 1]

    per_tile = S5_ROWS * S5_TILE // S5_WIDTH
    for b in range(nb):
        ys = []
        for k in range(S5_WIDTH // S5_TILE):
            slabs = range(k * per_tile, (k + 1) * per_tile)
            parts = [xr_ref[b, j * S5_PITCH:j * S5_PITCH + S5_TS, :].astype(BF16) for j in slabs]
            parts += [xi_ref[b, j * S5_PITCH:j * S5_PITCH + S5_TS, :].astype(BF16) for j in slabs]
            ys.append(jnp.dot(jnp.concatenate(parts, axis=1), c_ref[k], preferred_element_type=F32))
        y = jnp.concatenate(ys, axis=1) + d_ref[...] * u_ref[b]
        g = 0.5 * y * (1.0 + jnp.tanh(math.sqrt(2.0 / math.pi) * (y + 0.044715 * (y * y * y))))
        z = jnp.dot(g.astype(BF16), wglu_ref[...], preferred_element_type=F32) + bglu_ref[...]
        o_ref[b] = (g * jax.nn.sigmoid(z)).astype(BF16)


def _s5(u, ar, ai, bre, bim, cmat, dskip, wglu, bglu):
    nb, s, _ = u.shape
    blk = pl.BlockSpec((nb, S5_TS, S5_WIDTH), lambda i: (0, i, 0))
    return pl.pallas_call(
        _s5_body,
        grid=(s // S5_TS,),
        in_specs=[blk, _const_spec(ar.shape), _const_spec(ai.shape), _const_spec(bre.shape),
                  _const_spec(bim.shape), _const_spec(cmat.shape), _const_spec((1, S5_WIDTH)),
                  _const_spec(wglu.shape), _const_spec((1, S5_WIDTH))],
        out_specs=blk,
        out_shape=jax.ShapeDtypeStruct((nb, s, S5_WIDTH), BF16),
        scratch_shapes=[pltpu.VMEM((nb, S5_ROWS * S5_PITCH, LANES), F32),
                        pltpu.VMEM((nb, S5_ROWS * S5_PITCH, LANES), F32),
                        pltpu.VMEM((nb, S5_ROWS, LANES), F32),
                        pltpu.VMEM((nb, S5_ROWS, LANES), F32)],
        compiler_params=_params(("arbitrary",)),
        name="s5",
    )(u, ar, ai, bre, bim, cmat, dskip.reshape(1, -1), wglu, bglu.reshape(1, -1))


def _flash_t(k_ref, vt_ref, qt_ref, qi, scr, k_block, v_block):
    s0, s1, p0, p1, a0, a1, x0, x1, m_ref, acc_ref = scr
    strips = tuple(range(ATT_LANES // ATT_STRIP))

    def q_off(c):
        return (c * ATT_STRIP) % ATT_TQ

    lower = tuple(c for c in strips if q_off(c) < ATT_TK)
    upper = tuple(c for c in strips if q_off(c) >= ATT_TK)
    m_ref[...] = jnp.full(m_ref.shape, -jnp.inf, F32)
    acc_ref[...] = jnp.zeros(acc_ref.shape, F32)
    p1[...] = jnp.zeros(p1.shape, jnp.uint32)
    a1[...] = jnp.ones(a1.shape, F32)
    ones = jnp.ones((ATT_SUM_ROWS, ATT_TK), BF16)

    def scores(j, s_ref, x_ref, which):
        runs = []
        for c in which:
            if runs and runs[-1][-1] == c - 1 and k_block(runs[-1][0]) == k_block(c):
                runs[-1].append(c)
            else:
                runs.append([c])
        for run in runs:
            sl = slice(run[0] * ATT_STRIP, (run[-1] + 1) * ATT_STRIP)
            kb = k_ref[0, pl.ds(j * ATT_TK, ATT_TK), k_block(run[0]) * LANES:(k_block(run[0]) + 1) * LANES]
            s = jnp.dot(kb, qt_ref[:, sl], preferred_element_type=F32)
            s_ref[:, sl] = s
            x_ref[:, sl] = jnp.max(s, axis=0, keepdims=True)

    def softmax(s_ref, x_ref, p_ref, a_ref, which, key_off):
        for c in which:
            for t in range(ATT_STRIP // LANES):
                sl = slice(c * ATT_STRIP + t * LANES, c * ATT_STRIP + (t + 1) * LANES)
                s = s_ref[:, sl]
                if key_off is None:
                    s_max = x_ref[:, sl]
                else:
                    kpos = lax.broadcasted_iota(jnp.int32, s.shape, 0) + key_off
                    qpos = lax.broadcasted_iota(jnp.int32, s.shape, 1) + (q_off(c) + t * LANES)
                    s = jnp.where(kpos <= qpos, s, -jnp.inf)
                    s_max = jnp.max(s, axis=0, keepdims=True)
                m_old = m_ref[:, sl]
                m_new = jnp.maximum(m_old, s_max)
                a_ref[:, sl] = jnp.exp2(m_old - m_new)
                m_ref[:, sl] = m_new
                p_ref[:, sl] = pltpu.bitcast(jnp.exp2(s - m_new).astype(BF16), jnp.uint32)

    def value(pair, half, p_ref, a_ref, which):
        for c in which:
            sl = slice(c * ATT_STRIP, (c + 1) * ATT_STRIP)
            vt = vt_ref[0, pair, v_block(c) * LANES:(v_block(c) + 1) * LANES, half * ATT_TK:(half + 1) * ATT_TK]
            p = pltpu.bitcast(p_ref[:, sl], BF16)
            pv = jnp.dot(jnp.concatenate([vt, ones], axis=0), p, preferred_element_type=F32)
            acc_ref[:, sl] = a_ref[:, sl] * acc_ref[:, sl] + pv

    scores(0, s0, x0, strips)

    def pair_step(i, carry):
        softmax(s0, x0, p0, a0, strips, None)
        scores(2 * i + 1, s1, x1, strips)
        value(jnp.maximum(i - 1, 0), 1, p1, a1, strips)
        softmax(s1, x1, p1, a1, strips, None)
        scores(2 * i + 2, s0, x0, strips)
        value(i, 0, p0, a0, strips)
        return carry

    lax.fori_loop(0, qi, pair_step, 0)
    scores(2 * qi + 1, s1, x1, upper)
    softmax(s0, x0, p0, a0, lower, 0)
    softmax(s0, x0, p0, a0, upper, None)
    value(jnp.maximum(qi - 1, 0), 1, p1, a1, strips)
    softmax(s1, x1, p1, a1, upper, ATT_TK)
    value(qi, 0, p0, a0, strips)
    value(qi, 1, p1, a1, upper)
    return acc_ref[:LANES, :ATT_LANES] / acc_ref[LANES:LANES + 1, :ATT_LANES]


def _mla_attn_body(q_ref, k_ref, vt_ref, o_ref, *scr):
    qi = pl.program_id(2)
    qt_ref = scr[0]
    heads = ATT_LANES // ATT_TQ
    for h in range(heads):
        qt_ref[:, h * ATT_TQ:(h + 1) * ATT_TQ] = q_ref[0, :, h * LANES:(h + 1) * LANES].astype(F32).T.astype(BF16)
    per_head = ATT_TQ // ATT_STRIP
    ot = _flash_t(k_ref, vt_ref, qt_ref, qi, scr[1:], lambda c: c // per_head, lambda c: c // (2 * per_head))
    row = lax.broadcasted_iota(jnp.int32, (LANES, ATT_TQ), 0)
    for r in range(heads // 2):
        lo = ot[:, 2 * r * ATT_TQ:(2 * r + 1) * ATT_TQ]
        hi = ot[:, (2 * r + 1) * ATT_TQ:(2 * r + 2) * ATT_TQ]
        o_ref[0, :, r * LANES:(r + 1) * LANES] = jnp.where(row < MLA_V, lo, hi).T.astype(BF16)


def _attn_specs(s, qk_width, v_width):
    n = ATT_LANES
    row = pltpu.VMEM((1, n), F32)
    return dict(
        in_specs=[pl.BlockSpec((1, ATT_TQ, qk_width), lambda b, h, i: (b, i, h)),
                  pl.BlockSpec((1, s, qk_width), lambda b, h, i: (b, 0, h)),
                  pl.BlockSpec((1, s // ATT_TQ, v_width, ATT_TQ), lambda b, h, i: (b, 0, h, 0))],
        out_specs=pl.BlockSpec((1, ATT_TQ, v_width), lambda b, h, i: (b, i, h)),
        scratch_shapes=[pltpu.VMEM((LANES, n), BF16),
                        pltpu.VMEM((ATT_TK, n + ATT_PAD), F32), pltpu.VMEM((ATT_TK, n + ATT_PAD), F32),
                        pltpu.VMEM((ATT_TK // 2, n + ATT_PAD), jnp.uint32),
                        pltpu.VMEM((ATT_TK // 2, n + ATT_PAD), jnp.uint32),
                        row, row, row, row, row, pltpu.VMEM((LANES + ATT_SUM_ROWS, n + ATT_PAD), F32)],
        compiler_params=_params(("parallel", "parallel", "arbitrary")))


def _mla_attn(q, k, vt):
    nb, s, _ = q.shape
    heads = ATT_LANES // ATT_TQ
    return pl.pallas_call(
        _mla_attn_body,
        grid=(nb, MLA_HEADS // heads, s // ATT_TQ),
        out_shape=jax.ShapeDtypeStruct((nb, s, MLA_HEADS * MLA_V), BF16),
        name="mla_attn", **_attn_specs(s, heads * LANES, heads * MLA_V),
    )(q, k, vt)


def _diff_attn_body(lam_ref, q_ref, k_ref, vt_ref, sub_ref, o_ref, *scr, out_scale):
    qi = pl.program_id(2)
    qt_ref = scr[0]
    heads = ATT_LANES // (2 * ATT_TQ)
    for g in range(heads):
        qt = q_ref[0, :, g * LANES:(g + 1) * LANES].astype(F32).T
        row = lax.broadcasted_iota(jnp.int32, qt.shape, 0)
        zero = jnp.zeros_like(qt)
        qt_ref[:, 2 * g * ATT_TQ:(2 * g + 1) * ATT_TQ] = jnp.where(row < DIFF_HEAD_DIM, qt, zero).astype(BF16)
        qt_ref[:, (2 * g + 1) * ATT_TQ:(2 * g + 2) * ATT_TQ] = jnp.where(row < DIFF_HEAD_DIM, zero, qt).astype(BF16)
    per_head = 2 * ATT_TQ // ATT_STRIP
    ot = _flash_t(k_ref, vt_ref, qt_ref, qi, scr[1:], lambda c: c // per_head, lambda c: c // per_head)
    for g in range(heads):
        o = (ot[:, 2 * g * ATT_TQ:(2 * g + 1) * ATT_TQ]
             - lam_ref[0, 0] * ot[:, (2 * g + 1) * ATT_TQ:(2 * g + 2) * ATT_TQ]).T
        o_ref[0, :, g * LANES:(g + 1) * LANES] = (_rms(o, sub_ref[...]) * out_scale).astype(BF16)


def _diff_attn(lam, q, k, vt, subln, out_scale):
    nb, s, _ = q.shape
    heads = ATT_LANES // (2 * ATT_TQ)
    spec = _attn_specs(s, heads * LANES, heads * LANES)
    spec["in_specs"] = ([pl.BlockSpec(memory_space=pltpu.SMEM)] + spec["in_specs"]
                        + [pl.BlockSpec((1, LANES), lambda b, h, i: (0, 0))])
    return pl.pallas_call(
        functools.partial(_diff_attn_body, out_scale=out_scale),
        grid=(nb, DIFF_HEADS // heads, s // ATT_TQ),
        out_shape=jax.ShapeDtypeStruct((nb, s, DIFF_HEADS * 2 * DIFF_HEAD_DIM), BF16),
        name="diff_attn", **spec,
    )(lam, q, k, vt, subln.reshape(1, -1))


def _rope_table(positions, rot_dim, theta, first_lo, period, dup=False):
    half = rot_dim // 2
    inv = theta ** (-jnp.arange(half, dtype=F32) * 2.0 / rot_dim)
    ang = inv[:, None] * positions.astype(F32).reshape(1, -1)
    cos, sin = jnp.cos(ang), jnp.sin(ang)
    rel = np.arange(LANES) % period - first_lo
    rotated = ((rel >= 0) & (rel < rot_dim))[:, None]
    freq = np.where(rotated[:, 0], rel % half, 0)
    sign = np.where(rel < half, -1.0, 1.0).astype(np.float32)[:, None]
    copy = ((rel >= rot_dim) & (rel < rot_dim + half) & dup)[:, None]
    cos_t = jnp.where(rotated, cos[freq], np.where(copy, 0.0, 1.0).astype(np.float32))
    sin_t = jnp.where(rotated, sin[freq] * sign, 0.0)
    return cos_t.T, sin_t.T


def _s5_discretize(lam_re, lam_im, log_dt, b_re, b_im):
    dt = jnp.exp(log_dt)[:, None]
    mag = jnp.exp(lam_re * dt)
    ang = lam_im * dt
    ab_re = mag * jnp.cos(ang)
    ab_im = mag * jnp.sin(ang)
    den = lam_re * lam_re + lam_im * lam_im
    nr = ab_re - 1.0
    f_re = (nr * lam_re + ab_im * lam_im) / den
    f_im = (ab_im * lam_re - nr * lam_im) / den
    bb_re = f_re[..., None] * b_re - f_im[..., None] * b_im
    bb_im = f_re[..., None] * b_im + f_im[..., None] * b_re
    return ab_re, ab_im, bb_re, bb_im


def _block_diag_in(bb):
    eye = jnp.eye(S5_GROUPS, dtype=F32)
    return jnp.einsum('gph,gk->ghkp', bb, eye).reshape(S5_WIDTH, S5_FEATS)


def _block_diag_out(c):
    eye = jnp.eye(S5_GROUPS, dtype=F32)
    return jnp.einsum('ghp,gk->gpkh', c, eye).reshape(S5_FEATS, S5_WIDTH)


def _s5_in_tiles(bb):
    full = _block_diag_in(bb)
    tiles = [full[(n * S5_TILE // S5_EXPAND // S5_TILE) * S5_TILE:, n * S5_TILE:(n + 1) * S5_TILE][:S5_TILE]
             for n in range(S5_FEATS // S5_TILE)]
    return jnp.stack(tiles).astype(BF16)


def _s5_out_tiles(c_re, c_im):
    full_re, full_im = _block_diag_out(c_re), -_block_diag_out(c_im)
    rows = S5_TILE * S5_EXPAND
    tiles = [jnp.concatenate([full_re[k * rows:(k + 1) * rows, k * S5_TILE:(k + 1) * S5_TILE],
                              full_im[k * rows:(k + 1) * rows, k * S5_TILE:(k + 1) * S5_TILE]], axis=0)
             for k in range(S5_WIDTH // S5_TILE)]
    return jnp.stack(tiles).astype(BF16)


def _pad_heads(w, heads, width, lo, hi):
    kdim = w.shape[0]
    w = w.reshape(kdim, heads, width)[:, :, lo:hi]
    w = jnp.pad(w, ((0, 0), (0, 0), (0, LANES - (hi - lo))))
    return w.reshape(kdim, heads * LANES)


def kernel(x, positions, ffn_norm, ffn_w_gu, ffn_w_down, ev_norm, ev_w_in, s5_lambda_re, s5_lambda_im, s5_log_dt, s5_b_re, s5_b_im, s5_c_re, s5_c_im, s5_d, s5_w_glu, s5_b_glu, mla_q_norm, mla_w_uq, mla_kv_norm, mla_w_ukv, ev_w_out, od_norm, od_w_in, diff_lq1, diff_lk1, diff_lq2, diff_lk2, diff_subln, od_w_out, final_norm):
    nb, s, d = x.shape
    t = nb * s
    xt = x.reshape(t, d)
    wgu = ffn_w_gu.astype(BF16)
    wdn = ffn_w_down.astype(BF16)

    xt = _ffn(xt, [], ffn_norm[0, 0], wgu, wdn, 0, 0)

    cos_m, sin_m = _rope_table(positions, MLA_ROPE, MLA_ROPE_THETA, MLA_NOPE, LANES, dup=True)
    w_in = ev_w_in[0]
    o_kr = S5_WIDTH + MLA_Q_LORA + MLA_KV_LORA
    half_m = MLA_ROPE // 2
    w_kr = jnp.concatenate([w_in[:, o_kr:], w_in[:, o_kr:o_kr + half_m]], axis=1)
    w_kr = jnp.pad(w_kr, ((0, 0), (MLA_NOPE, LANES - MLA_NOPE - MLA_ROPE - half_m)))
    win_p = jnp.concatenate([w_in[:, :o_kr], w_kr], axis=1).astype(BF16)
    wuq = mla_w_uq[0].reshape(MLA_Q_LORA, MLA_HEADS, MLA_NOPE + MLA_ROPE)
    wuq = jnp.concatenate([wuq, wuq[:, :, MLA_NOPE:MLA_NOPE + half_m]], axis=2)
    wuq_p = _pad_heads(wuq.reshape(MLA_Q_LORA, -1), MLA_HEADS, MLA_NOPE + MLA_ROPE + half_m, 0,
                       MLA_NOPE + MLA_ROPE + half_m).astype(BF16)
    wuk_p = _pad_heads(mla_w_ukv[0], MLA_HEADS, MLA_NOPE + MLA_V, 0, MLA_NOPE).astype(BF16)
    wuv = mla_w_ukv[0].reshape(MLA_KV_LORA, MLA_HEADS, MLA_NOPE + MLA_V)[:, :, MLA_NOPE:]
    wuvt = wuv.reshape(MLA_KV_LORA, MLA_HEADS * MLA_V).T.astype(BF16)
    u, q, k, vt = _even_prep(xt, ev_norm[0], win_p, mla_q_norm[0], wuq_p, mla_kv_norm[0], wuk_p, wuvt,
                             cos_m, sin_m, nb)

    ab_re, ab_im, bb_re, bb_im = _s5_discretize(s5_lambda_re[0], s5_lambda_im[0], s5_log_dt[0], s5_b_re[0], s5_b_im[0])
    s5_out = _s5(u.reshape(nb, s, S5_WIDTH), ab_re.reshape(S5_ROWS, LANES), ab_im.reshape(S5_ROWS, LANES),
                 _s5_in_tiles(bb_re), _s5_in_tiles(bb_im), _s5_out_tiles(s5_c_re[0], s5_c_im[0]),
                 s5_d[0].reshape(-1), s5_w_glu[0].astype(BF16), s5_b_glu[0])

    o = _mla_attn(q.reshape(nb, s, -1), k.reshape(nb, s, -1), vt)
    w_out = ev_w_out[0].astype(BF16)
    xt = _ffn(xt, [(s5_out.reshape(t, S5_WIDTH), w_out[:S5_WIDTH]), (o.reshape(t, -1), w_out[S5_WIDTH:])],
              ffn_norm[0, 1], wgu, wdn, 0, 1)

    xt = _ffn(xt, [], ffn_norm[1, 0], wgu, wdn, 1, 0)
    cos_t, sin_t = _rope_table(positions, DIFF_ROT, ROPE_THETA, 0, DIFF_HEAD_DIM)
    width = DIFF_HEADS * 2 * DIFF_HEAD_DIM
    w_in = od_w_in[0]
    q, k, vt = _odd_prep(xt, od_norm[0], w_in[:, :2 * width].astype(BF16), w_in[:, 2 * width:].T.astype(BF16),
                         cos_t, sin_t, nb)
    lam_init = 0.8 - 0.6 * math.exp(-0.3 * 1)
    lam = (jnp.exp(jnp.sum(diff_lq1[0] * diff_lk1[0])) - jnp.exp(jnp.sum(diff_lq2[0] * diff_lk2[0])) + lam_init)
    o = _diff_attn(lam.reshape(1, 1), q.reshape(nb, s, -1), k.reshape(nb, s, -1), vt, diff_subln[0],
                   1.0 - lam_init)
    xt = _ffn(xt, [(o.reshape(t, -1), od_w_out[0].astype(BF16))], ffn_norm[1, 1], wgu, wdn, 1, 1,
              gfinal=final_norm)
    return xt.reshape(nb, s, d)
```
